```python
import jax
import jax.numpy as jnp
from jax import lax
import numpy as np


D_MODEL = 4096
BATCH = 1
SEQ = 16384
DEPTH = 2

CTX_LEN = 256
GRID_W = 64
HEAD_DIM = 128
GQA_HEADS = 16
GQA_KV_HEADS = 4
NA_HEADS = 8
NA_WIN_ROWS = 8
NA_WIN_COLS = 16
FOURIER_GROUPS = 8
FOURIER_GROUP_DIM = 128
GQA_WIDTH = GQA_HEADS * HEAD_DIM
GQA_KV_WIDTH = GQA_KV_HEADS * HEAD_DIM
NA_WIDTH = NA_HEADS * HEAD_DIM
FOURIER_WIDTH = FOURIER_GROUPS * FOURIER_GROUP_DIM
MIX_WIDTH = GQA_WIDTH + NA_WIDTH + FOURIER_WIDTH
PROJ_SIZES = (GQA_WIDTH, GQA_KV_WIDTH, GQA_KV_WIDTH, NA_WIDTH, NA_WIDTH, NA_WIDTH, FOURIER_WIDTH)
IN_WIDTH = GQA_WIDTH + 2 * GQA_KV_WIDTH + 3 * NA_WIDTH + FOURIER_WIDTH
Q_BLOCK = 128
ROPE_THETA = 10000.0
N_EXPERTS = 64
TOP_K = 8
N_EXPERT_GROUPS = 8
TOPK_GROUPS = 4
EXPERT_FF = 384
SHARED_FF = 384
ROUTED_SCALE = 2.5
MOE_BLOCK = 128
EPS = 1e-6

kernel_name = 'hybrid_gqa_natten_fnet_moe_dit'


def rmsnorm(x, g):
    xf = x.astype(jnp.float32)
    y = xf * lax.rsqrt(jnp.mean(xf * xf, axis=-1, keepdims=True) + EPS)
    return (y * g.astype(jnp.float32)).astype(x.dtype)


def axial_rope(n_tokens):
    pos = jnp.arange(n_tokens, dtype=jnp.int32)
    row = (pos // GRID_W).astype(jnp.float32)
    col = (pos % GRID_W).astype(jnp.float32)
    n_freq = HEAD_DIM // 4
    inv_freq = ROPE_THETA ** (-jnp.arange(n_freq, dtype=jnp.float32) / n_freq)
    ang = jnp.concatenate([row[:, None] * inv_freq, col[:, None] * inv_freq], axis=-1)
    return jnp.cos(ang), jnp.sin(ang)


def apply_rope(x, cos, sin):
    xf = x.astype(jnp.float32).reshape(x.shape[:-1] + (HEAD_DIM // 2, 2))
    x1, x2 = xf[..., 0], xf[..., 1]
    c = cos[None, :, None, :]
    s = sin[None, :, None, :]
    out = jnp.stack([x1 * c - x2 * s, x1 * s + x2 * c], axis=-1)
    return out.reshape(x.shape).astype(x.dtype)


def split_proj(p):
    parts = []
    start = 0
    for size in PROJ_SIZES:
        parts.append(p[..., start:start + size])
        start += size
    return parts


def heads(t, n):
    return t.reshape(t.shape[:2] + (n, HEAD_DIM))


def blocked_attention(q, k, v):
    B, Tq, Hq, hd = q.shape
    Hkv = k.shape[2]
    G = Hq // Hkv
    nb = Tq // Q_BLOCK
    qb = q.reshape(B, nb, Q_BLOCK, Hkv, G, hd).swapaxes(0, 1)
    scale = hd ** -0.5

    def one_block(qi):
        s = jnp.einsum('bqkgd,bskd->bkgqs', qi, k, preferred_element_type=jnp.float32) * scale
        p = jax.nn.softmax(s, axis=-1).astype(v.dtype)
        return jnp.einsum('bkgqs,bskd->bqkgd', p, v)

    o = lax.map(one_block, qb)
    return o.swapaxes(0, 1).reshape(B, Tq, Hq * hd)


def neighbourhood_attention(q, k, v, k_ctx, v_ctx, rel_bias):
    B, N, H, hd = q.shape
    rows = N // GRID_W
    win_r = min(NA_WIN_ROWS, rows)
    scale = hd ** -0.5
    n_ctx = k_ctx.shape[1]
    qg = q.reshape(B, rows, GRID_W, H, hd).swapaxes(0, 1)
    kg = k.reshape(B, rows, GRID_W, H, hd)
    vg = v.reshape(B, rows, GRID_W, H, hd)
    col = jnp.arange(GRID_W, dtype=jnp.int32)
    col_start = jnp.clip(col - NA_WIN_COLS // 2, 0, GRID_W - NA_WIN_COLS)
    col_idx = col_start[:, None] + jnp.arange(NA_WIN_COLS, dtype=jnp.int32)[None, :]
    col_off = col_idx - col[:, None] + NA_WIN_COLS - 1

    def row_block(args):
        r, q_row = args
        r_start = jnp.clip(r - NA_WIN_ROWS // 2, 0, rows - win_r)
        k_nb = lax.dynamic_slice_in_dim(kg, r_start, win_r, axis=1)[:, :, col_idx]
        v_nb = lax.dynamic_slice_in_dim(vg, r_start, win_r, axis=1)[:, :, col_idx]
        row_off = r_start + jnp.arange(win_r, dtype=jnp.int32) - r + NA_WIN_ROWS - 1
        bias = rel_bias[:, row_off[None, :, None], col_off[:, None, :]]
        s_nb = jnp.einsum('bqhd,biqjhd->bhqij', q_row, k_nb, preferred_element_type=jnp.float32) * scale
        s_nb = s_nb + bias[None].astype(jnp.float32)
        s_ctx = jnp.einsum('bqhd,bshd->bhqs', q_row, k_ctx, preferred_element_type=jnp.float32) * scale
        s = jnp.concatenate([s_ctx, s_nb.reshape(B, H, GRID_W, win_r * NA_WIN_COLS)], axis=-1)
        p = jax.nn.softmax(s, axis=-1).astype(v.dtype)
        p_ctx = p[..., :n_ctx]
        p_nb = p[..., n_ctx:].reshape(B, H, GRID_W, win_r, NA_WIN_COLS)
        return (jnp.einsum('bhqs,bshd->bqhd', p_ctx, v_ctx)
                + jnp.einsum('bhqij,biqjhd->bqhd', p_nb, v_nb))

    out = lax.map(row_block, (jnp.arange(rows, dtype=jnp.int32), qg))
    return out.swapaxes(0, 1).reshape(B, N, H * hd)


def fourier_mix(f):
    B, T, _ = f.shape
    fg = f.reshape(B, T, FOURIER_GROUPS, FOURIER_GROUP_DIM).astype(jnp.float32)
    y = jnp.real(jnp.fft.fft2(fg, axes=(1, 3)))
    return y.reshape(B, T, FOURIER_WIDTH).astype(f.dtype)


def merge_groups(o_gqa, o_na, o_four, g_out, w_out):
    m = jnp.concatenate([
        rmsnorm(o_gqa, g_out[:GQA_WIDTH]),
        rmsnorm(o_na, g_out[GQA_WIDTH:GQA_WIDTH + NA_WIDTH]),
        rmsnorm(o_four, g_out[GQA_WIDTH + NA_WIDTH:])], axis=-1)
    return m @ w_out


def swiglu(h, wg, wu, wd):
    return (jax.nn.silu(h @ wg) * (h @ wu)) @ wd


def route(h, router_w, router_bias):
    T = h.shape[0]
    scores = jax.nn.sigmoid(jnp.dot(h, router_w, preferred_element_type=jnp.float32))
    biased = scores + router_bias.astype(jnp.float32)
    grp = biased.reshape(T, N_EXPERT_GROUPS, N_EXPERTS // N_EXPERT_GROUPS)
    grp_score = jnp.sum(lax.top_k(grp, 2)[0], axis=-1)
    top_grp = lax.top_k(grp_score, TOPK_GROUPS)[1]
    grp_keep = jnp.any(top_grp[:, :, None] == jnp.arange(N_EXPERT_GROUPS)[None, None, :], axis=1)
    masked = jnp.where(grp_keep[:, :, None], grp, -jnp.inf).reshape(T, N_EXPERTS)
    idx = lax.top_k(masked, TOP_K)[1]
    w = jnp.take_along_axis(scores, idx, axis=1)
    w = w / jnp.sum(w, axis=-1, keepdims=True) * ROUTED_SCALE
    return idx, w


def routed_experts(h, idx, w, w_gate, w_up, w_down):
    T, D = h.shape
    E = w_gate.shape[0]
    TK = T * TOP_K
    flat_e = idx.reshape(TK).astype(jnp.int32)
    order = jnp.argsort(flat_e).astype(jnp.int32)
    sorted_e = flat_e[order]
    counts = jnp.zeros((E,), jnp.int32).at[flat_e].add(1)
    padded = (counts + MOE_BLOCK - 1) // MOE_BLOCK * MOE_BLOCK
    pad_end = jnp.cumsum(padded)
    pad_start = pad_end - padded
    grp_start = jnp.cumsum(counts) - counts
    dest = pad_start[sorted_e] + jnp.arange(TK, dtype=jnp.int32) - grp_start[sorted_e]
    n_blocks = -(-TK // MOE_BLOCK) + E
    n_rows = n_blocks * MOE_BLOCK
    row_tok = jnp.full((n_rows,), T, jnp.int32).at[dest].set(order // TOP_K)
    row_w = jnp.zeros((n_rows,), h.dtype).at[dest].set(w.reshape(TK)[order].astype(h.dtype))
    block_e = jnp.minimum(
        jnp.searchsorted(pad_end, jnp.arange(n_blocks, dtype=jnp.int32) * MOE_BLOCK, side='right'), E - 1)
    h_pad = jnp.concatenate([h, jnp.zeros((1, D), h.dtype)], axis=0)

    def one_block(args):
        tok, e, wr = args
        xb = h_pad[tok]
        y = swiglu(xb, w_gate[e], w_up[e], w_down[e])
        return y * wr[:, None]

    y = lax.map(one_block, (row_tok.reshape(n_blocks, MOE_BLOCK), block_e,
                            row_w.reshape(n_blocks, MOE_BLOCK)))
    return jax.ops.segment_sum(y.reshape(n_rows, D), row_tok, num_segments=T + 1)[:T]


def moe_ffn(h, router_w, router_bias, w_gate, w_up, w_down, s_gate, s_up, s_down):
    idx, w = route(h, router_w, router_bias)
    return swiglu(h, s_gate, s_up, s_down) + routed_experts(h, idx, w, w_gate, w_up, w_down)


def setup_inputs(seed: int = 0) -> dict:
    key = jax.random.key(seed)
    ks = jax.random.split(key, 24)
    f32 = jnp.float32
    D = D_MODEL

    def nrm(k, shape, scale):
        return jax.random.normal(k, shape, f32) * scale

    def gain(k, shape):
        return 1.0 + 0.02 * jax.random.normal(k, shape, f32)

    return {
        'x': nrm(ks[0], (BATCH, SEQ, D), 1.0),
        'c': nrm(ks[1], (BATCH, D), 1.0),
        'ctx': nrm(ks[2], (BATCH, CTX_LEN, D), 1.0),
        'c_ctx': nrm(ks[3], (D,), 1.0),
        'ada_w': nrm(ks[4], (DEPTH, D, 6 * D), 0.5 * D ** -0.5),
        'ada_b': nrm(ks[5], (DEPTH, 6 * D), 0.01),
        'norm1_g': gain(ks[6], (DEPTH, D)),
        'w_in': nrm(ks[7], (DEPTH, D, IN_WIDTH), D ** -0.5),
        'q_norm_g': gain(ks[8], (DEPTH, HEAD_DIM)),
        'k_norm_g': gain(ks[9], (DEPTH, HEAD_DIM)),
        'na_rel_bias': nrm(ks[10], (DEPTH, NA_HEADS, 2 * NA_WIN_ROWS - 1, 2 * NA_WIN_COLS - 1), 0.1),
        'out_norm_g': gain(ks[11], (DEPTH, MIX_WIDTH)),
        'w_out': nrm(ks[12], (DEPTH, MIX_WIDTH, D), MIX_WIDTH ** -0.5),
        'norm2_g': gain(ks[13], (DEPTH, D)),
        'router_w': nrm(ks[14], (DEPTH, D, N_EXPERTS), D ** -0.5),
        'router_bias': nrm(ks[15], (DEPTH, N_EXPERTS), 0.01),
        'exp_w_gate': nrm(ks[16], (DEPTH, N_EXPERTS, D, EXPERT_FF), D ** -0.5),
        'exp_w_up': nrm(ks[17], (DEPTH, N_EXPERTS, D, EXPERT_FF), D ** -0.5),
        'exp_w_down': nrm(ks[18], (DEPTH, N_EXPERTS, EXPERT_FF, D), EXPERT_FF ** -0.5),
        'shared_w_gate': nrm(ks[19], (DEPTH, D, SHARED_FF), D ** -0.5),
        'shared_w_up': nrm(ks[20], (DEPTH, D, SHARED_FF), D ** -0.5),
        'shared_w_down': nrm(ks[21], (DEPTH, SHARED_FF, D), SHARED_FF ** -0.5),
        'final_g': gain(ks[22], (D,)),
    }


def reference(x, c, ctx, c_ctx, ada_w, ada_b, norm1_g, w_in, q_norm_g, k_norm_g, na_rel_bias,
              out_norm_g, w_out, norm2_g, router_w, router_bias, exp_w_gate, exp_w_up, exp_w_down,
              shared_w_gate, shared_w_up, shared_w_down, final_g):
    B, N, D = x.shape
    L = ctx.shape[1]
    cos, sin = axial_rope(N)
    silu_c = jax.nn.silu(c)
    silu_cc = jax.nn.silu(c_ctx)[None, :]
    xc = ctx
    for layer in range(DEPTH):
        last = layer == DEPTH - 1
        mod = silu_c @ ada_w[layer] + ada_b[layer]
        mod_c = silu_cc @ ada_w[layer] + ada_b[layer]
        sh1, sc1, g1, sh2, sc2, g2 = jnp.split(mod[:, None, :], 6, axis=-1)
        csh1, csc1, cg1, csh2, csc2, cg2 = jnp.split(mod_c[:, None, :], 6, axis=-1)

        h = rmsnorm(x, norm1_g[layer]) * (1 + sc1) + sh1
        hc = rmsnorm(xc, norm1_g[layer]) * (1 + csc1) + csh1
        qa, ka, va, qn, kn, vn, f = split_proj(h @ w_in[layer])
        qa_c, ka_c, va_c, qn_c, kn_c, vn_c, f_c = split_proj(hc @ w_in[layer])

        qa = apply_rope(rmsnorm(heads(qa, GQA_HEADS), q_norm_g[layer]), cos, sin)
        ka = apply_rope(rmsnorm(heads(ka, GQA_KV_HEADS), k_norm_g[layer]), cos, sin)
        va = heads(va, GQA_KV_HEADS)
        ka_c = rmsnorm(heads(ka_c, GQA_KV_HEADS), k_norm_g[layer])
        va_c = heads(va_c, GQA_KV_HEADS)
        qn, kn, vn = heads(qn, NA_HEADS), heads(kn, NA_HEADS), heads(vn, NA_HEADS)
        kn_c, vn_c = heads(kn_c, NA_HEADS), heads(vn_c, NA_HEADS)

        o_gqa = blocked_attention(qa, jnp.concatenate([ka_c, ka], axis=1),
                                  jnp.concatenate([va_c, va], axis=1))
        o_na = neighbourhood_attention(qn, kn, vn, kn_c, vn_c, na_rel_bias[layer])
        o_four = fourier_mix(f)
        x = x + g1 * merge_groups(o_gqa, o_na, o_four, out_norm_g[layer], w_out[layer])

        if not last:
            qa_c = rmsnorm(heads(qa_c, GQA_HEADS), q_norm_g[layer])
            oc = merge_groups(blocked_attention(qa_c, ka_c, va_c),
                              blocked_attention(heads(qn_c, NA_HEADS), kn_c, vn_c),
                              fourier_mix(f_c), out_norm_g[layer], w_out[layer])
            xc = xc + cg1 * oc

        h2 = (rmsnorm(x, norm2_g[layer]) * (1 + sc2) + sh2).reshape(B * N, D)
        if not last:
            h2c = (rmsnorm(xc, norm2_g[layer]) * (1 + csc2) + csh2).reshape(B * L, D)
            h2 = jnp.concatenate([h2, h2c], axis=0)
        y = moe_ffn(h2, router_w[layer], router_bias[layer], exp_w_gate[layer], exp_w_up[layer],
                    exp_w_down[layer], shared_w_gate[layer], shared_w_up[layer], shared_w_down[layer])
        x = x + g2 * y[:B * N].reshape(B, N, D)
        if not last:
            xc = xc + cg2 * y[B * N:].reshape(B, L, D)
    return rmsnorm(x, final_g)
```

```python
import functools
import math

import numpy as np
import jax
import jax.numpy as jnp
from jax import lax
from jax.experimental import pallas as pl
from jax.experimental.pallas import tpu as pltpu

F32 = jnp.float32
BF16 = jnp.bfloat16
U32 = jnp.uint32
I32 = jnp.int32

HEAD_DIM = 128
GRID_W = 64
GQA_HEADS = 16
GQA_KV_HEADS = 4
NA_HEADS = 8
NA_WIN_ROWS = 8
NA_WIN_COLS = 16
FOURIER_GROUPS = 8
GQA_WIDTH = GQA_HEADS * HEAD_DIM
GQA_KV_WIDTH = GQA_KV_HEADS * HEAD_DIM
NA_WIDTH = NA_HEADS * HEAD_DIM
FOURIER_WIDTH = FOURIER_GROUPS * HEAD_DIM
MIX_WIDTH = GQA_WIDTH + NA_WIDTH + FOURIER_WIDTH
IN_WIDTH = GQA_WIDTH + 2 * GQA_KV_WIDTH + 3 * NA_WIDTH + FOURIER_WIDTH
QA_OFF = 0
KA_OFF = GQA_WIDTH
VA_OFF = KA_OFF + GQA_KV_WIDTH
QN_OFF = VA_OFF + GQA_KV_WIDTH
KN_OFF = QN_OFF + NA_WIDTH
VN_OFF = KN_OFF + NA_WIDTH
F_OFF = VN_OFF + NA_WIDTH
ROPE_THETA = 10000.0
N_EXPERTS = 64
TOP_K = 8
N_EXPERT_GROUPS = 8
TOPK_GROUPS = 4
ROUTED_SCALE = 2.5
EPS = 1e-6
LOG2E = 1.4426950408889634
MASK_VALUE = -1e30

VMEM_LIMIT = 56 * 1024 * 1024
MOE_ROWS = 256
COMBINE_TOKENS = 64


def _tile(m, pref, mult=8):
    for t in range(min(pref, m), 0, -1):
        if m % t == 0 and t % mult == 0:
            return t
    return m


def _params(sem):
    return pltpu.CompilerParams(dimension_semantics=sem, vmem_limit_bytes=VMEM_LIMIT)


def _sigmoid(x):
    return 1.0 / (1.0 + jnp.exp(-x))


def _pack_halves(y):
    w = y.shape[1] // 2
    lo = lax.bitcast_convert_type(y[:, :w].astype(BF16).astype(F32), U32) >> 16
    hi = lax.bitcast_convert_type(y[:, w:].astype(BF16).astype(F32), U32) & jnp.uint32(0xFFFF0000)
    return lo | hi


def _unpack_halves(p):
    lo = lax.bitcast_convert_type(p << 16, F32)
    hi = lax.bitcast_convert_type(p & jnp.uint32(0xFFFF0000), F32)
    return lo, hi


def _adaln_kernel(s_ref, w_ref, b_ref, o_ref):
    d = w_ref.shape[1]
    tn = w_ref.shape[2]
    ch = _tile(d, 256)

    def body(r, acc):
        a0, a1 = acc
        r0 = pl.multiple_of(r * ch, ch)
        w = w_ref[0, pl.ds(r0, ch), :]
        s = s_ref[pl.ds(r0, ch), :]
        s = s * _sigmoid(s)
        p0 = (w * s[:, 0:1]).reshape(ch // 8, 8, tn).sum(axis=0)
        p1 = (w * s[:, 1:2]).reshape(ch // 8, 8, tn).sum(axis=0)
        return a0 + p0, a1 + p1

    z = jnp.zeros((8, tn), F32)
    a0, a1 = lax.fori_loop(0, d // ch, body, (z, z))
    o_ref[0, 0:1, :] = a0.sum(axis=0, keepdims=True) + b_ref[0]
    o_ref[0, 1:2, :] = a1.sum(axis=0, keepdims=True) + b_ref[0]


def _adaln(s_cols, ada_w, ada_b):
    depth, d, n6 = ada_w.shape
    tn = _tile(n6, 512, 128)
    return pl.pallas_call(
        _adaln_kernel,
        grid=(depth, n6 // tn),
        in_specs=[
            pl.BlockSpec((d, 2), lambda l, j: (0, 0)),
            pl.BlockSpec((1, d, tn), lambda l, j: (l, 0, j)),
            pl.BlockSpec((1, 1, tn), lambda l, j: (l, 0, j)),
        ],
        out_specs=pl.BlockSpec((1, 2, tn), lambda l, j: (l, 0, j)),
        out_shape=jax.ShapeDtypeStruct((depth, 2, n6), F32),
        compiler_params=_params(("arbitrary", "arbitrary")),
        name="adaln",
    )(s_cols, ada_w, ada_b.reshape(depth, 1, n6))


def _normmod_kernel(x_ref, g_ref, sc_ref, sh_ref, *o_refs):
    x = x_ref[...]
    ms = jnp.mean(x * x, axis=-1, keepdims=True)
    y = x * lax.rsqrt(ms + EPS) * g_ref[...]
    h = y * (1.0 + sc_ref[...]) + sh_ref[...]
    o_refs[0][...] = h.astype(BF16)
    if len(o_refs) > 1:
        o_refs[1][...] = _pack_halves(h)


def _normmod(x, g, sc, sh, packed=False):
    m, d = x.shape
    tm = _tile(m, 256)
    vec = pl.BlockSpec((1, d), lambda i: (0, 0))
    out_shape = [jax.ShapeDtypeStruct((m, d), BF16)]
    out_specs = [pl.BlockSpec((tm, d), lambda i: (i, 0))]
    if packed:
        out_shape.append(jax.ShapeDtypeStruct((m, d // 2), U32))
        out_specs.append(pl.BlockSpec((tm, d // 2), lambda i: (i, 0)))
    res = pl.pallas_call(
        _normmod_kernel,
        grid=(m // tm,),
        in_specs=[pl.BlockSpec((tm, d), lambda i: (i, 0)), vec, vec, vec],
        out_specs=out_specs,
        out_shape=out_shape,
        compiler_params=_params(("arbitrary",)),
        name="normmod",
    )(x, g.reshape(1, d), sc.reshape(1, d), sh.reshape(1, d))
    return res if packed else res[0]


def _mm_kernel(a_ref, b_ref, o_ref):
    o_ref[...] = jnp.dot(a_ref[...], b_ref[...], preferred_element_type=F32).astype(o_ref.dtype)


def _matmul(a, b, out_dtype, col_block_off=0, n_out=None, tm_pref=1024, tn_pref=1024):
    m, k = a.shape
    n_out = b.shape[1] if n_out is None else n_out
    tm = _tile(m, tm_pref)
    tn = _tile(n_out, tn_pref, 128)
    return pl.pallas_call(
        _mm_kernel,
        grid=(m // tm, n_out // tn),
        in_specs=[
            pl.BlockSpec((tm, k), lambda i, j: (i, 0)),
            pl.BlockSpec((k, tn), lambda i, j: (0, j + col_block_off)),
        ],
        out_specs=pl.BlockSpec((tm, tn), lambda i, j: (i, j)),
        out_shape=jax.ShapeDtypeStruct((m, n_out), out_dtype),
        compiler_params=_params(("arbitrary", "arbitrary")),
        name="matmul",
    )(a, b)


def _head_norm_rope(x, g, c, se, so):
    ms = jnp.mean(x * x, axis=-1, keepdims=True)
    y = x * lax.rsqrt(ms + EPS) * g
    return y * c + pltpu.roll(y, HEAD_DIM - 1, 1) * se + pltpu.roll(y, 1, 1) * so


def _qprep_kernel(p_ref, c_ref, se_ref, so_ref, g_ref, o_ref, *, nheads):
    c, se, so, g = c_ref[...], se_ref[...], so_ref[...], g_ref[...]
    for h in range(nheads):
        sl = slice(h * HEAD_DIM, (h + 1) * HEAD_DIM)
        o_ref[:, sl] = _head_norm_rope(p_ref[:, sl].astype(F32), g, c, se, so).astype(o_ref.dtype)


def _qprep(p, tabs, g):
    m = p.shape[0]
    tq = _tile(m, 512)
    tab = pl.BlockSpec((tq, HEAD_DIM), lambda i: (i, 0))
    return pl.pallas_call(
        functools.partial(_qprep_kernel, nheads=GQA_HEADS),
        grid=(m // tq,),
        in_specs=[pl.BlockSpec((tq, GQA_WIDTH), lambda i: (i, QA_OFF // GQA_WIDTH)), tab, tab, tab,
                  pl.BlockSpec((1, HEAD_DIM), lambda i: (0, 0))],
        out_specs=pl.BlockSpec((tq, GQA_WIDTH), lambda i: (i, 0)),
        out_shape=jax.ShapeDtypeStruct((m, GQA_WIDTH), BF16),
        compiler_params=_params(("arbitrary",)),
        name="qprep",
    )(p, *tabs, g.reshape(1, HEAD_DIM))


def _kvprep_kernel(pc_ref, pl_ref, c_ref, se_ref, so_ref, g_ref, o_ref, *, n_ctx_blocks):
    i = pl.program_id(0)
    c, se, so, g = c_ref[...], se_ref[...], so_ref[...], g_ref[...]

    def run(p_ref):
        for h in range(GQA_KV_HEADS):
            sl = slice(h * HEAD_DIM, (h + 1) * HEAD_DIM)
            o_ref[:, sl] = _head_norm_rope(p_ref[:, sl].astype(F32), g, c, se, so).astype(o_ref.dtype)
        o_ref[:, GQA_KV_WIDTH:] = p_ref[:, GQA_KV_WIDTH:]

    @pl.when(i < n_ctx_blocks)
    def _():
        run(pc_ref)

    @pl.when(i >= n_ctx_blocks)
    def _():
        run(pl_ref)


def _kvprep(p_ctx, p_lat, tabs, g):
    l, n = p_ctx.shape[0], p_lat.shape[0]
    t = _tile(math.gcd(l, n), 256)
    ncb = l // t
    w = 2 * GQA_KV_WIDTH
    cb = KA_OFF // w
    tab = pl.BlockSpec((t, HEAD_DIM), lambda i: (i, 0))
    return pl.pallas_call(
        functools.partial(_kvprep_kernel, n_ctx_blocks=ncb),
        grid=((l + n) // t,),
        in_specs=[pl.BlockSpec((t, w), lambda i: (jnp.minimum(i, ncb - 1), cb)),
                  pl.BlockSpec((t, w), lambda i: (jnp.maximum(i - ncb, 0), cb)),
                  tab, tab, tab, pl.BlockSpec((1, HEAD_DIM), lambda i: (0, 0))],
        out_specs=pl.BlockSpec((t, w), lambda i: (i, 0)),
        out_shape=jax.ShapeDtypeStruct((l + n, w), BF16),
        compiler_params=_params(("arbitrary",)),
        name="kvprep",
    )(p_ctx, p_lat, *tabs, g.reshape(1, HEAD_DIM))


def _flash_kernel(q_ref, k_ref, v_ref, o_ref, qs_ref, m_ref, l_ref, acc_ref, *, group, scale):
    j = pl.program_id(2)
    tq = q_ref.shape[0]

    @pl.when(j == 0)
    def _():
        for h in range(group):
            sl = slice(h * HEAD_DIM, (h + 1) * HEAD_DIM)
            qs_ref[h * tq:(h + 1) * tq, :] = (q_ref[:, sl].astype(F32) * scale).astype(BF16)
        m_ref[...] = jnp.full(m_ref.shape, -jnp.inf, F32)
        l_ref[...] = jnp.zeros(l_ref.shape, F32)
        acc_ref[...] = jnp.zeros(acc_ref.shape, F32)

    s = lax.dot_general(qs_ref[...], k_ref[...], (((1,), (1,)), ((), ())), preferred_element_type=F32)
    m_prev = m_ref[...]
    m_new = jnp.maximum(m_prev, jnp.max(s, axis=-1, keepdims=True))
    alpha = jnp.exp2(m_prev - m_new)
    p = jnp.exp2(s - m_new)
    l_ref[...] = alpha * l_ref[...] + jnp.sum(p, axis=-1, keepdims=True)
    acc_ref[...] = alpha * acc_ref[...] + jnp.dot(p.astype(BF16), v_ref[...], preferred_element_type=F32)
    m_ref[...] = m_new

    @pl.when(j == pl.num_programs(2) - 1)
    def _():
        inv = 1.0 / l_ref[...]
        for h in range(group):
            sl = slice(h * HEAD_DIM, (h + 1) * HEAD_DIM)
            o_ref[:, sl] = (acc_ref[h * tq:(h + 1) * tq, :] * inv[h * tq:(h + 1) * tq]).astype(o_ref.dtype)


def _flash(q, q_off, k, k_off, v, v_off, n_kv_heads, group, tq_pref=512, tk_pref=1280):
    nq, tk_total = q.shape[0], k.shape[0]
    tq = _tile(nq, tq_pref)
    tk = _tile(tk_total, tk_pref, 128)
    gw = group * HEAD_DIM
    scale = HEAD_DIM ** -0.5 * LOG2E
    return pl.pallas_call(
        functools.partial(_flash_kernel, group=group, scale=scale),
        grid=(n_kv_heads, nq // tq, tk_total // tk),
        in_specs=[
            pl.BlockSpec((tq, gw), lambda g, i, j: (i, q_off // gw + g)),
            pl.BlockSpec((tk, HEAD_DIM), lambda g, i, j: (j, k_off // HEAD_DIM + g)),
            pl.BlockSpec((tk, HEAD_DIM), lambda g, i, j: (j, v_off // HEAD_DIM + g)),
        ],
        out_specs=pl.BlockSpec((tq, gw), lambda g, i, j: (i, g)),
        out_shape=jax.ShapeDtypeStruct((nq, n_kv_heads * gw), BF16),
        scratch_shapes=[
            pltpu.VMEM((group * tq, HEAD_DIM), BF16),
            pltpu.VMEM((group * tq, 1), F32),
            pltpu.VMEM((group * tq, 1), F32),
            pltpu.VMEM((group * tq, HEAD_DIM), F32),
        ],
        compiler_params=_params(("arbitrary", "arbitrary", "arbitrary")),
        name="flash",
    )(q, k, v)


NA_QROWS = 8
NA_BAND = 16


def _na_kernel(q_ref, k0, k1, k2, k3, v0, v1, v2, v3, kc_ref, vc_ref, tb_ref, o_ref, kb_ref, vb_ref,
               *, rows):
    b = pl.program_id(0)
    quarter = NA_BAND * GRID_W // 4
    for d, (kr, vr) in enumerate(((k0, v0), (k1, v1), (k2, v2), (k3, v3))):
        kb_ref[d * quarter:(d + 1) * quarter, :] = kr[...]
        vb_ref[d * quarter:(d + 1) * quarter, :] = vr[...]
    band_start = jnp.clip(NA_QROWS * b - NA_WIN_ROWS // 2, 0, rows - NA_BAND)
    scale = HEAD_DIM ** -0.5
    win = NA_WIN_ROWS * GRID_W

    def body(a, carry):
        r = NA_QROWS * b + a
        r_start = jnp.clip(r - NA_WIN_ROWS // 2, 0, rows - NA_WIN_ROWS)
        dd = r - r_start
        koff = pl.multiple_of((r_start - band_start) * GRID_W, GRID_W)
        qoff = pl.multiple_of(a * GRID_W, GRID_W)
        q_a = q_ref[pl.ds(qoff, GRID_W), :]
        kw = kb_ref[pl.ds(koff, win), :]
        vw = vb_ref[pl.ds(koff, win), :]
        for h in range(NA_HEADS):
            sl = slice(h * HEAD_DIM, (h + 1) * HEAD_DIM)
            qh = q_a[:, sl]
            dn = (((1,), (1,)), ((), ()))
            s_nb = lax.dot_general(qh, kw[:, sl], dn, preferred_element_type=F32) * scale + tb_ref[h, dd]
            s_c = lax.dot_general(qh, kc_ref[:, sl], dn, preferred_element_type=F32) * scale
            m = jnp.maximum(jnp.max(s_nb, axis=-1, keepdims=True), jnp.max(s_c, axis=-1, keepdims=True))
            p_nb = jnp.exp(s_nb - m)
            p_c = jnp.exp(s_c - m)
            l = jnp.sum(p_nb, axis=-1, keepdims=True) + jnp.sum(p_c, axis=-1, keepdims=True)
            o = (jnp.dot(p_c.astype(BF16), vc_ref[:, sl], preferred_element_type=F32)
                 + jnp.dot(p_nb.astype(BF16), vw[:, sl], preferred_element_type=F32))
            o_ref[pl.ds(qoff, GRID_W), sl] = (o / l).astype(o_ref.dtype)
        return carry

    lax.fori_loop(0, NA_QROWS, body, 0)


def _na_bias_table(rel_bias):
    qc = jnp.arange(GRID_W, dtype=I32)
    kc = jnp.arange(GRID_W, dtype=I32)
    col_start = jnp.clip(qc - NA_WIN_COLS // 2, 0, GRID_W - NA_WIN_COLS)
    valid = (kc[None, :] >= col_start[:, None]) & (kc[None, :] < col_start[:, None] + NA_WIN_COLS)
    coff = jnp.clip(kc[None, :] - qc[:, None] + NA_WIN_COLS - 1, 0, 2 * NA_WIN_COLS - 2)
    i = jnp.arange(NA_WIN_ROWS, dtype=I32)
    dd = jnp.arange(NA_WIN_ROWS, dtype=I32)
    ro = i[None, :] + NA_WIN_ROWS - 1 - dd[:, None]
    tb = rel_bias[:, ro[:, :, None, None], coff[None, None, :, :]].astype(F32)
    tb = jnp.where(valid[None, None, None], tb, MASK_VALUE)
    return tb.transpose(0, 1, 3, 2, 4).reshape(NA_HEADS, NA_WIN_ROWS, GRID_W, NA_WIN_ROWS * GRID_W)


def _na(p_lat, p_ctx, rel_bias):
    n, l = p_lat.shape[0], p_ctx.shape[0]
    rows = n // GRID_W
    assert rows % NA_QROWS == 0 and rows >= NA_BAND
    tq = NA_QROWS * GRID_W
    quarter = NA_BAND * GRID_W // 4
    nquart = n // quarter

    def band_block(d, col):
        return pl.BlockSpec((quarter, NA_WIDTH),
                            lambda b: (jnp.clip(2 * b - 1, 0, nquart - 4) + d, col))

    kcol, vcol = KN_OFF // NA_WIDTH, VN_OFF // NA_WIDTH
    tb = _na_bias_table(rel_bias)
    return pl.pallas_call(
        functools.partial(_na_kernel, rows=rows),
        grid=(rows // NA_QROWS,),
        in_specs=[pl.BlockSpec((tq, NA_WIDTH), lambda b: (b, QN_OFF // NA_WIDTH))]
        + [band_block(d, kcol) for d in range(4)]
        + [band_block(d, vcol) for d in range(4)]
        + [pl.BlockSpec((l, NA_WIDTH), lambda b: (0, kcol)),
           pl.BlockSpec((l, NA_WIDTH), lambda b: (0, vcol)),
           pl.BlockSpec(tb.shape, lambda b: (0, 0, 0, 0))],
        out_specs=pl.BlockSpec((tq, NA_WIDTH), lambda b: (b, 0)),
        out_shape=jax.ShapeDtypeStruct((n, NA_WIDTH), BF16),
        scratch_shapes=[pltpu.VMEM((NA_BAND * GRID_W, NA_WIDTH), BF16),
                        pltpu.VMEM((NA_BAND * GRID_W, NA_WIDTH), BF16)],
        compiler_params=_params(("arbitrary",)),
        name="natten",
    )(p_lat, *([p_lat] * 8), p_ctx, p_ctx, tb)


def _dft_cs(n):
    k = np.arange(n, dtype=np.float64)
    ang = 2.0 * np.pi * np.outer(k, k) / n
    return np.cos(ang), np.sin(ang)


def _f1_kernel(x_ref, da_ref, db_ref, twr_ref, twi_ref, u_ref, *, tb, n_hi):
    w = FOURIER_WIDTH
    for j in range(tb):
        x = x_ref[:, j, :].astype(BF16)
        zs = [jnp.dot(x[:, g * HEAD_DIM:(g + 1) * HEAD_DIM], da_ref[...], preferred_element_type=F32)
              for g in range(FOURIER_GROUPS)]
        zr = jnp.concatenate([z[:, :HEAD_DIM] for z in zs], axis=1)
        zi = jnp.concatenate([z[:, HEAD_DIM:] for z in zs], axis=1)
        zst = jnp.concatenate([zr, zi], axis=0).astype(BF16)
        u = jnp.dot(db_ref[...], zst, preferred_element_type=F32)
        ur, ui = u[:n_hi], u[n_hi:]
        tr, ti = twr_ref[j], twi_ref[j]
        u_ref[:, j, :w] = ur * tr - ui * ti
        u_ref[:, j, w:] = ur * ti + ui * tr


def _f2_kernel(u_ref, dc_ref, y_ref, *, ta):
    w = FOURIER_WIDTH
    for j in range(ta):
        u = u_ref[j]
        ust = jnp.concatenate([u[:, :w], u[:, w:]], axis=0).astype(BF16)
        y_ref[:, j, :] = jnp.dot(dc_ref[...], ust, preferred_element_type=F32)


def _fourier(xf):
    n, w = xf.shape
    n_lo = 128
    n_hi = n // n_lo
    assert n_hi * n_lo == n and n_hi % 8 == 0
    cc, sc = _dft_cs(HEAD_DIM)
    da = jnp.asarray(np.concatenate([cc, -sc], axis=1), BF16)
    cn, sn = _dft_cs(n_hi)
    db = jnp.asarray(np.block([[cn, sn], [-sn, cn]]), BF16)
    ang = 2.0 * np.pi * np.outer(np.arange(n_lo), np.arange(n_hi)) / n
    twr = jnp.asarray(np.cos(ang)[:, :, None], F32)
    twi = jnp.asarray(-np.sin(ang)[:, :, None], F32)
    cl, sl = _dft_cs(n_lo)
    dc = jnp.asarray(np.concatenate([cl, sl], axis=1), BF16)
    tb = 8
    u = pl.pallas_call(
        functools.partial(_f1_kernel, tb=tb, n_hi=n_hi),
        grid=(n_lo // tb,),
        in_specs=[pl.BlockSpec((n_hi, tb, w), lambda j: (0, j, 0)),
                  pl.BlockSpec(da.shape, lambda j: (0, 0)),
                  pl.BlockSpec(db.shape, lambda j: (0, 0)),
                  pl.BlockSpec((tb, n_hi, 1), lambda j: (j, 0, 0)),
                  pl.BlockSpec((tb, n_hi, 1), lambda j: (j, 0, 0))],
        out_specs=pl.BlockSpec((n_hi, tb, 2 * w), lambda j: (0, j, 0)),
        out_shape=jax.ShapeDtypeStruct((n_hi, n_lo, 2 * w), F32),
        compiler_params=_params(("arbitrary",)),
        name="fourier_stage1",
    )(xf.reshape(n_hi, n_lo, w), da, db, twr, twi)
    ta = 8
    y = pl.pallas_call(
        functools.partial(_f2_kernel, ta=ta),
        grid=(n_hi // ta,),
        in_specs=[pl.BlockSpec((ta, n_lo, 2 * w), lambda i: (i, 0, 0)),
                  pl.BlockSpec(dc.shape, lambda i: (0, 0))],
        out_specs=pl.BlockSpec((n_lo, ta, w), lambda i: (0, i, 0)),
        out_shape=jax.ShapeDtypeStruct((n_lo, n_hi, w), F32),
        compiler_params=_params(("arbitrary",)),
        name="fourier_stage2",
    )(u, dc)
    return y.reshape(n, w)


def _fourier_small_kernel(x_ref, dch_ref, dpos_ref, y_ref):
    x = x_ref[...].astype(BF16)
    for g in range(FOURIER_GROUPS):
        sl = slice(g * HEAD_DIM, (g + 1) * HEAD_DIM)
        a = jnp.dot(x[:, sl], dch_ref[...], preferred_element_type=F32)
        ast = jnp.concatenate([a[:, :HEAD_DIM], a[:, HEAD_DIM:]], axis=0).astype(BF16)
        y_ref[:, sl] = jnp.dot(dpos_ref[...], ast, preferred_element_type=F32)


def _fourier_small(xf):
    t, w = xf.shape
    cc, sc = _dft_cs(HEAD_DIM)
    dch = jnp.asarray(np.concatenate([cc, sc], axis=1), BF16)
    cp, sp = _dft_cs(t)
    dpos = jnp.asarray(np.concatenate([cp, -sp], axis=1), BF16)
    return pl.pallas_call(
        _fourier_small_kernel,
        grid=(1,),
        in_specs=[pl.BlockSpec((t, w), lambda i: (0, 0)),
                  pl.BlockSpec(dch.shape, lambda i: (0, 0)),
                  pl.BlockSpec(dpos.shape, lambda i: (0, 0))],
        out_specs=pl.BlockSpec((t, w), lambda i: (0, 0)),
        out_shape=jax.ShapeDtypeStruct((t, w), F32),
        compiler_params=_params(("arbitrary",)),
        name="fourier_small",
    )(xf, dch, dpos)


def _outproj_kernel(og_ref, on_ref, of_ref, g_ref, w_ref, x_ref, gate_ref, o_ref, m_ref):
    @pl.when(pl.program_id(1) == 0)
    def _():
        def nrm(v, g):
            vf = v.astype(F32)
            return (vf * lax.rsqrt(jnp.mean(vf * vf, axis=-1, keepdims=True) + EPS) * g).astype(BF16)

        a, b = GQA_WIDTH, GQA_WIDTH + NA_WIDTH
        m_ref[:, :a] = nrm(og_ref[...], g_ref[:, :a])
        m_ref[:, a:b] = nrm(on_ref[...], g_ref[:, a:b])
        m_ref[:, b:] = nrm(of_ref[...], g_ref[:, b:])

    acc = jnp.dot(m_ref[...], w_ref[...], preferred_element_type=F32)
    o_ref[...] = x_ref[...] + gate_ref[...] * acc


def _outproj(o_gqa, o_na, o_four, g_out, w_out, x, gate):
    m, d = x.shape
    tm = _tile(m, 512)
    tn = _tile(d, 1024, 128)
    return pl.pallas_call(
        _outproj_kernel,
        grid=(m // tm, d // tn),
        in_specs=[pl.BlockSpec((tm, GQA_WIDTH), lambda i, j: (i, 0)),
                  pl.BlockSpec((tm, NA_WIDTH), lambda i, j: (i, 0)),
                  pl.BlockSpec((tm, FOURIER_WIDTH), lambda i, j: (i, 0)),
                  pl.BlockSpec((1, MIX_WIDTH), lambda i, j: (0, 0)),
                  pl.BlockSpec((MIX_WIDTH, tn), lambda i, j: (0, j)),
                  pl.BlockSpec((tm, tn), lambda i, j: (i, j)),
                  pl.BlockSpec((1, tn), lambda i, j: (0, j))],
        out_specs=pl.BlockSpec((tm, tn), lambda i, j: (i, j)),
        out_shape=jax.ShapeDtypeStruct((m, d), F32),
        scratch_shapes=[pltpu.VMEM((tm, MIX_WIDTH), BF16)],
        compiler_params=_params(("arbitrary", "arbitrary")),
        name="outproj",
    )(o_gqa, o_na, o_four, g_out.reshape(1, MIX_WIDTH), w_out, x, gate.reshape(1, d))


def _router_kernel(h_ref, rw_ref, rb_ref, idx_ref, wt_ref, rank_ref, cnt_ref, carry_ref):
    step = pl.program_id(0)

    @pl.when(step == 0)
    def _():
        carry_ref[...] = jnp.zeros(carry_ref.shape, F32)

    tm = h_ref.shape[0]
    e = N_EXPERTS
    gsz = e // N_EXPERT_GROUPS
    neg = -jnp.inf
    logits = jnp.dot(h_ref[...], rw_ref[...], preferred_element_type=F32)
    scores = _sigmoid(logits)
    biased = scores + rb_ref[...]
    lane = lax.broadcasted_iota(I32, (tm, e), 1).astype(F32)
    lane_grp = lax.broadcasted_iota(I32, (tm, e), 1) // gsz

    def first_argmax(v):
        m = jnp.max(v, axis=-1, keepdims=True)
        return m, jnp.min(jnp.where(v == m, lane, float(e)), axis=-1, keepdims=True)

    gs = []
    for g in range(N_EXPERT_GROUPS):
        vg = jnp.where(lane_grp == g, biased, neg)
        m1, i1 = first_argmax(vg)
        m2 = jnp.max(jnp.where(lane == i1, neg, vg), axis=-1, keepdims=True)
        gs.append(m1 + m2)
    masked = jnp.full((tm, e), neg, F32)
    for g in range(N_EXPERT_GROUPS):
        ahead = jnp.zeros((tm, 1), F32)
        for g2 in range(N_EXPERT_GROUPS):
            if g2 == g:
                continue
            better = (gs[g2] > gs[g]) | ((gs[g2] == gs[g]) & (g2 < g))
            ahead = ahead + jnp.where(better, 1.0, 0.0)
        masked = jnp.where((lane_grp == g) & (ahead < TOPK_GROUPS), biased, masked)
    sel = jnp.zeros((tm, e), F32)
    idxs, ws = [], []
    cur = masked
    for _ in range(TOP_K):
        _, ik = first_argmax(cur)
        hit = lane == ik
        ws.append(jnp.sum(jnp.where(hit, scores, 0.0), axis=-1, keepdims=True))
        cur = jnp.where(hit, neg, cur)
        sel = sel + jnp.where(hit, 1.0, 0.0)
        idxs.append(ik)
    wsum = ws[0]
    for wk in ws[1:]:
        wsum = wsum + wk
    rr = lax.broadcasted_iota(I32, (tm, tm), 0)
    cc = lax.broadcasted_iota(I32, (tm, tm), 1)
    tri = jnp.where(cc < rr, 1.0, 0.0).astype(BF16)
    pref = jnp.dot(tri, sel.astype(BF16), preferred_element_type=F32) + carry_ref[...]
    ranks = [jnp.sum(jnp.where(lane == ik, pref, 0.0), axis=-1, keepdims=True) for ik in idxs]
    carry_ref[...] = carry_ref[...] + jnp.sum(sel, axis=0, keepdims=True)

    lane_o = lax.broadcasted_iota(I32, (tm, 128), 1)
    o_idx = jnp.zeros((tm, 128), F32)
    o_w = jnp.zeros((tm, 128), F32)
    o_rank = jnp.zeros((tm, 128), F32)
    for k in range(TOP_K):
        o_idx = jnp.where(lane_o == k, idxs[k], o_idx)
        o_w = jnp.where(lane_o == k, ws[k] / wsum * ROUTED_SCALE, o_w)
        o_rank = jnp.where(lane_o == k, ranks[k], o_rank)
    idx_ref[...] = o_idx.astype(I32)
    wt_ref[...] = o_w
    rank_ref[...] = o_rank.astype(I32)
    cnt_ref[...] = jnp.broadcast_to(carry_ref[...], cnt_ref.shape)


def _router(h, rw, rb):
    t, d = h.shape
    tm = _tile(t, 256)
    blk = pl.BlockSpec((tm, 128), lambda i: (i, 0))
    idx, wt, rank, cnt = pl.pallas_call(
        _router_kernel,
        grid=(t // tm,),
        in_specs=[pl.BlockSpec((tm, d), lambda i: (i, 0)),
                  pl.BlockSpec((d, N_EXPERTS), lambda i: (0, 0)),
                  pl.BlockSpec((1, N_EXPERTS), lambda i: (0, 0))],
        out_specs=[blk, blk, blk, pl.BlockSpec((8, N_EXPERTS), lambda i: (0, 0))],
        out_shape=[jax.ShapeDtypeStruct((t, 128), I32), jax.ShapeDtypeStruct((t, 128), F32),
                   jax.ShapeDtypeStruct((t, 128), I32), jax.ShapeDtypeStruct((8, N_EXPERTS), F32)],
        scratch_shapes=[pltpu.VMEM((1, N_EXPERTS), F32)],
        compiler_params=_params(("arbitrary",)),
        name="router",
    )(h, rw, rb.reshape(1, N_EXPERTS))
    return idx[:, :TOP_K], wt[:, :TOP_K], rank[:, :TOP_K], cnt[0].astype(I32)


def _gather_pipeline(step, n_used, idx_hbm, idx_smem, idx_sem, src_hbm, bufs, buf_sem, issue_rows):
    slot = step % 2
    nslot = 1 - slot

    def idx_copy(s, sl):
        return pltpu.make_async_copy(idx_hbm.at[s], idx_smem.at[sl], idx_sem.at[sl])

    def buf_wait(sl):
        pltpu.make_async_copy(bufs.at[sl], bufs.at[sl], buf_sem.at[sl]).wait()

    @pl.when(step == 0)
    def _():
        idx_copy(0, 0).start()
        idx_copy(0, 0).wait()
        issue_rows(0, lambda r: idx_smem[0, r])

        @pl.when(n_used > 1)
        def _():
            idx_copy(1, 1).start()

    @pl.when(step + 1 < n_used)
    def _():
        idx_copy(step + 1, nslot).wait()
        issue_rows(nslot, lambda r: idx_smem[nslot, r])

    @pl.when(step + 2 < n_used)
    def _():
        idx_copy(step + 2, slot).start()

    buf_wait(slot)
    return slot


def _expert_kernel(be_ref, nu_ref, rt_hbm, hp_hbm, wgu_ref, wd_ref, o_ref, idx_smem, xbuf, idx_sem, x_sem):
    b = pl.program_id(0)
    n_used = nu_ref[0]
    bm = xbuf.shape[1]
    dh = xbuf.shape[2]
    ff = wd_ref.shape[1]

    @pl.when(b < n_used)
    def _():
        def issue_rows(slot, read_index):
            def body(r, carry):
                tok = read_index(r)
                pltpu.make_async_copy(hp_hbm.at[pl.ds(tok, 1), :], xbuf.at[slot, pl.ds(r, 1), :],
                                      x_sem.at[slot]).start()
                return carry

            lax.fori_loop(0, bm, body, 0)

        slot = _gather_pipeline(b, n_used, rt_hbm, idx_smem, idx_sem, hp_hbm, xbuf, x_sem, issue_rows)
        lo, hi = _unpack_halves(xbuf[slot])
        h = (jnp.dot(lo.astype(BF16), wgu_ref[0, :dh, :], preferred_element_type=F32)
             + jnp.dot(hi.astype(BF16), wgu_ref[0, dh:, :], preferred_element_type=F32))
        gate, up = h[:, :ff], h[:, ff:]
        act = (gate * _sigmoid(gate) * up).astype(BF16)
        y = jnp.dot(act, wd_ref[0], preferred_element_type=F32)
        o_ref[...] = _pack_halves(y)

    @pl.when(b >= n_used)
    def _():
        o_ref[...] = jnp.zeros(o_ref.shape, o_ref.dtype)


def _experts(hp, row_tok, block_e, n_used, wgu, wd):
    n_blocks, bm = row_tok.shape
    dh = hp.shape[1]
    d = 2 * dh
    ff = wd.shape[1]
    grid_spec = pltpu.PrefetchScalarGridSpec(
        num_scalar_prefetch=2,
        grid=(n_blocks,),
        in_specs=[pl.BlockSpec(memory_space=pl.ANY),
                  pl.BlockSpec(memory_space=pl.ANY),
                  pl.BlockSpec((1, d, 2 * ff), lambda b, be, nu: (be[b], 0, 0)),
                  pl.BlockSpec((1, ff, d), lambda b, be, nu: (be[b], 0, 0))],
        out_specs=pl.BlockSpec((bm, dh), lambda b, be, nu: (b, 0)),
        scratch_shapes=[pltpu.SMEM((2, bm), I32),
                        pltpu.VMEM((2, bm, dh), U32),
                        pltpu.SemaphoreType.DMA((2,)),
                        pltpu.SemaphoreType.DMA((2,))],
    )
    return pl.pallas_call(
        _expert_kernel,
        grid_spec=grid_spec,
        out_shape=jax.ShapeDtypeStruct((n_blocks * bm, dh), U32),
        compiler_params=_params(("arbitrary",)),
        name="experts",
    )(block_e, n_used, row_tok, hp, wgu, wd)


def _shared_kernel(h_ref, wgu_ref, wd_ref, o_ref):
    ff = wd_ref.shape[0]
    h = jnp.dot(h_ref[...], wgu_ref[...], preferred_element_type=F32)
    gate, up = h[:, :ff], h[:, ff:]
    act = (gate * _sigmoid(gate) * up).astype(BF16)
    o_ref[...] = jnp.dot(act, wd_ref[...], preferred_element_type=F32).astype(o_ref.dtype)


def _shared(h, wgu, wd):
    t, d = h.shape
    ff = wd.shape[0]
    tm = _tile(t, 512)
    return pl.pallas_call(
        _shared_kernel,
        grid=(t // tm,),
        in_specs=[pl.BlockSpec((tm, d), lambda i: (i, 0)),
                  pl.BlockSpec((d, 2 * ff), lambda i: (0, 0)),
                  pl.BlockSpec((ff, d), lambda i: (0, 0))],
        out_specs=pl.BlockSpec((tm, d), lambda i: (i, 0)),
        out_shape=jax.ShapeDtypeStruct((t, d), BF16),
        compiler_params=_params(("arbitrary",)),
        name="shared_expert",
    )(h, wgu, wd)


def _combine_kernel(dest_hbm, ys_hbm, w_ref, ysh_ref, x_ref, gate_ref, *rest, final):
    if final:
        fg_ref, o_ref, idx_smem, gbuf, idx_sem, g_sem = rest
    else:
        o_ref, idx_smem, gbuf, idx_sem, g_sem = rest
    s = pl.program_id(0)
    n_steps = pl.num_programs(0)
    tt = gbuf.shape[2]
    dh = gbuf.shape[3]

    def issue_rows(slot, read_index):
        def body(t, carry):
            for k in range(TOP_K):
                row = read_index(t * TOP_K + k)
                pltpu.make_async_copy(ys_hbm.at[pl.ds(row, 1), :], gbuf.at[slot, k, pl.ds(t, 1), :],
                                      g_sem.at[slot]).start()
            return carry

        lax.fori_loop(0, tt, body, 0)

    slot = _gather_pipeline(s, n_steps, dest_hbm, idx_smem, idx_sem, ys_hbm, gbuf, g_sem, issue_rows)
    w = w_ref[...]
    acc_lo = jnp.zeros((tt, dh), F32)
    acc_hi = jnp.zeros((tt, dh), F32)
    for k in range(TOP_K):
        lo, hi = _unpack_halves(gbuf[slot, k])
        wk = w[:, k:k + 1]
        acc_lo = acc_lo + wk * lo
        acc_hi = acc_hi + wk * hi
    y = jnp.concatenate([acc_lo, acc_hi], axis=1) + ysh_ref[...].astype(F32)
    xn = x_ref[...] + gate_ref[...] * y
    if final:
        xn = xn * lax.rsqrt(jnp.mean(xn * xn, axis=-1, keepdims=True) + EPS) * fg_ref[...]
    o_ref[...] = xn


def _combine(dest, w, ys, ysh, x, gate, final_g=None):
    t, d = x.shape
    tt = _tile(t, COMBINE_TOKENS)
    dh = d // 2
    final = final_g is not None
    in_specs = [pl.BlockSpec(memory_space=pl.ANY),
                pl.BlockSpec(memory_space=pl.ANY),
                pl.BlockSpec((tt, TOP_K), lambda i: (i, 0)),
                pl.BlockSpec((tt, d), lambda i: (i, 0)),
                pl.BlockSpec((tt, d), lambda i: (i, 0)),
                pl.BlockSpec((1, d), lambda i: (0, 0))]
    args = [dest.reshape(t // tt, tt * TOP_K), ys, w, ysh, x, gate.reshape(1, d)]
    if final:
        in_specs.append(pl.BlockSpec((1, d), lambda i: (0, 0)))
        args.append(final_g.reshape(1, d))
    return pl.pallas_call(
        functools.partial(_combine_kernel, final=final),
        grid=(t // tt,),
        in_specs=in_specs,
        out_specs=pl.BlockSpec((tt, d), lambda i: (i, 0)),
        out_shape=jax.ShapeDtypeStruct((t, d), F32),
        scratch_shapes=[pltpu.SMEM((2, tt * TOP_K), I32),
                        pltpu.VMEM((2, TOP_K, tt, dh), U32),
                        pltpu.SemaphoreType.DMA((2,)),
                        pltpu.SemaphoreType.DMA((2,))],
        compiler_params=_params(("arbitrary",)),
        name="combine",
    )(*args)


def _moe_tables(idx, rank, counts, bm):
    t = idx.shape[0]
    e = N_EXPERTS
    tk = t * TOP_K
    padded = (counts + bm - 1) // bm * bm
    pad_end = jnp.cumsum(padded)
    pad_start = pad_end - padded
    dest = pad_start[idx] + rank
    n_blocks = -(-tk // bm) + e
    tok = jnp.broadcast_to(jnp.arange(t, dtype=I32)[:, None], (t, TOP_K))
    row_tok = jnp.zeros((n_blocks * bm,), I32).at[dest.reshape(tk)].set(tok.reshape(tk))
    block_e = jnp.minimum(
        jnp.searchsorted(pad_end, jnp.arange(n_blocks, dtype=I32) * bm, side='right'), e - 1).astype(I32)
    n_used = (pad_end[-1] // bm).astype(I32).reshape(1)
    return dest.astype(I32), row_tok.reshape(n_blocks, bm), block_e, n_used


def _rope_tables(n, n_ctx):
    pos = jnp.arange(n, dtype=I32)
    row = (pos // GRID_W).astype(F32)
    col = (pos % GRID_W).astype(F32)
    n_freq = HEAD_DIM // 4
    inv_freq = ROPE_THETA ** (-jnp.arange(n_freq, dtype=F32) / n_freq)
    ang = jnp.concatenate([row[:, None] * inv_freq, col[:, None] * inv_freq], axis=-1)
    cos = jnp.repeat(jnp.cos(ang), 2, axis=1)
    sin = jnp.repeat(jnp.sin(ang), 2, axis=1)
    even = (jnp.arange(HEAD_DIM) % 2 == 0)[None, :]
    c = jnp.concatenate([jnp.ones((n_ctx, HEAD_DIM), F32), cos], axis=0)
    se = jnp.concatenate([jnp.zeros((n_ctx, HEAD_DIM), F32), jnp.where(even, -sin, 0.0)], axis=0)
    so = jnp.concatenate([jnp.zeros((n_ctx, HEAD_DIM), F32), jnp.where(even, 0.0, sin)], axis=0)
    return c, se, so


def kernel(x, c, ctx, c_ctx, ada_w, ada_b, norm1_g, w_in, q_norm_g, k_norm_g, na_rel_bias, out_norm_g,
           w_out, norm2_g, router_w, router_bias, exp_w_gate, exp_w_up, exp_w_down, shared_w_gate,
           shared_w_up, shared_w_down, final_g):
    b, n, d = x.shape
    l = ctx.shape[1]
    depth = ada_w.shape[0]
    assert b == 1
    xl = x[0]
    xc = ctx[0]
    mod = _adaln(jnp.stack([c_ctx, c[0]], axis=1), ada_w, ada_b)
    tabs_all = _rope_tables(n, l)
    tabs_lat = tuple(t[l:] for t in tabs_all)
    tabs_ctx = tuple(t[:l] for t in tabs_all)

    for layer in range(depth):
        last = layer == depth - 1
        sh1, sc1, g1, sh2, sc2, g2 = [mod[layer, 1, i * d:(i + 1) * d] for i in range(6)]
        csh1, csc1, cg1, csh2, csc2, cg2 = [mod[layer, 0, i * d:(i + 1) * d] for i in range(6)]

        w_in_b = w_in[layer].astype(BF16)
        h = _normmod(xl, norm1_g[layer], sc1, sh1)
        hc = _normmod(xc, norm1_g[layer], csc1, csh1)
        p = _matmul(h, w_in_b, BF16, n_out=F_OFF)
        pc = _matmul(hc, w_in_b, BF16, n_out=F_OFF)
        f = _matmul(h, w_in_b, F32, col_block_off=F_OFF // FOURIER_WIDTH, n_out=FOURIER_WIDTH)

        qa = _qprep(p, tabs_lat, q_norm_g[layer])
        kv = _kvprep(pc, p, tabs_all, k_norm_g[layer])
        o_gqa = _flash(qa, 0, kv, 0, kv, GQA_KV_WIDTH, GQA_KV_HEADS, GQA_HEADS // GQA_KV_HEADS)
        o_na = _na(p, pc, na_rel_bias[layer])
        o_four = _fourier(f)
        w_out_b = w_out[layer].astype(BF16)
        xl = _outproj(o_gqa, o_na, o_four, out_norm_g[layer], w_out_b, xl, g1)

        if not last:
            fc = _matmul(hc, w_in_b, F32, col_block_off=F_OFF // FOURIER_WIDTH, n_out=FOURIER_WIDTH)
            qa_c = _qprep(pc, tabs_ctx, q_norm_g[layer])
            kv_c = kv[:l]
            oc_gqa = _flash(qa_c, 0, kv_c, 0, kv_c, GQA_KV_WIDTH, GQA_KV_HEADS, GQA_HEADS // GQA_KV_HEADS)
            oc_na = _flash(pc, QN_OFF, pc, KN_OFF, pc, VN_OFF, NA_HEADS, 1)
            oc_four = _fourier_small(fc)
            xc = _outproj(oc_gqa, oc_na, oc_four, out_norm_g[layer], w_out_b, xc, cg1)

        h2, h2p = _normmod(xl, norm2_g[layer], sc2, sh2, packed=True)
        if not last:
            h2c, h2cp = _normmod(xc, norm2_g[layer], csc2, csh2, packed=True)
            h2 = jnp.concatenate([h2, h2c], axis=0)
            h2p = jnp.concatenate([h2p, h2cp], axis=0)
        idx, wts, rank, counts = _router(h2, router_w[layer].astype(BF16), router_bias[layer])
        dest, row_tok, block_e, n_used = _moe_tables(idx, rank, counts, MOE_ROWS)
        wgu = jnp.concatenate([exp_w_gate[layer], exp_w_up[layer]], axis=-1).astype(BF16)
        ys = _experts(h2p, row_tok, block_e, n_used, wgu, exp_w_down[layer].astype(BF16))
        sgu = jnp.concatenate([shared_w_gate[layer], shared_w_up[layer]], axis=-1).astype(BF16)
        ysh = _shared(h2, sgu, shared_w_down[layer].astype(BF16))
        xl = _combine(dest[:n], wts[:n], ys, ysh[:n], xl, g2, final_g=final_g if last else None)
        if not last:
            xc = _combine(dest[n:], wts[n:], ys, ysh[n:], xc, cg2)
    return xl[None]
```

```python
import functools
import math

import numpy as np
import jax
import jax.numpy as jnp
from jax import lax
from jax.experimental import pallas as pl
from jax.experimental.pallas import tpu as pltpu

F32 = jnp.float32
BF16 = jnp.bfloat16
U32 = jnp.uint32
I32 = jnp.int32

HEAD_DIM = 128
GRID_W = 64
GQA_HEADS = 16
GQA_KV_HEADS = 4
NA_HEADS = 8
NA_WIN_ROWS = 8
NA_WIN_COLS = 16
FOURIER_GROUPS = 8
GQA_WIDTH = GQA_HEADS * HEAD_DIM
GQA_KV_WIDTH = GQA_KV_HEADS * HEAD_DIM
NA_WIDTH = NA_HEADS * HEAD_DIM
FOURIER_WIDTH = FOURIER_GROUPS * HEAD_DIM
MIX_WIDTH = GQA_WIDTH + NA_WIDTH + FOURIER_WIDTH
IN_WIDTH = GQA_WIDTH + 2 * GQA_KV_WIDTH + 3 * NA_WIDTH + FOURIER_WIDTH
QA_OFF = 0
KA_OFF = GQA_WIDTH
VA_OFF = KA_OFF + GQA_KV_WIDTH
QN_OFF = VA_OFF + GQA_KV_WIDTH
KN_OFF = QN_OFF + NA_WIDTH
VN_OFF = KN_OFF + NA_WIDTH
F_OFF = VN_OFF + NA_WIDTH
ROPE_THETA = 10000.0
N_EXPERTS = 64
TOP_K = 8
N_EXPERT_GROUPS = 8
TOPK_GROUPS = 4
ROUTED_SCALE = 2.5
EPS = 1e-6
LOG2E = 1.4426950408889634
MASK_VALUE = -1e30

VMEM_LIMIT = 56 * 1024 * 1024
MOE_ROWS = 256
COMBINE_TOKENS = 64


def _tile(m, pref, mult=8):
    for t in range(min(pref, m), 0, -1):
        if m % t == 0 and t % mult == 0:
            return t
    return m


def _params(sem):
    return pltpu.CompilerParams(dimension_semantics=sem, vmem_limit_bytes=VMEM_LIMIT)


def _sigmoid(x):
    return 1.0 / (1.0 + jnp.exp(-x))


def _pack_halves(y):
    w = y.shape[1] // 2
    lo = lax.bitcast_convert_type(y[:, :w].astype(BF16).astype(F32), U32) >> 16
    hi = lax.bitcast_convert_type(y[:, w:].astype(BF16).astype(F32), U32) & jnp.uint32(0xFFFF0000)
    return lo | hi


def _unpack_halves(p):
    lo = lax.bitcast_convert_type(p << 16, F32)
    hi = lax.bitcast_convert_type(p & jnp.uint32(0xFFFF0000), F32)
    return lo, hi


def _adaln_kernel(s_ref, w_ref, b_ref, o_ref):
    d = w_ref.shape[1]
    tn = w_ref.shape[2]
    ch = _tile(d, 256)

    def body(r, acc):
        a0, a1 = acc
        r0 = pl.multiple_of(r * ch, ch)
        w = w_ref[0, pl.ds(r0, ch), :]
        s = s_ref[pl.ds(r0, ch), :]
        s = s * _sigmoid(s)
        p0 = (w * s[:, 0:1]).reshape(ch // 8, 8, tn).sum(axis=0)
        p1 = (w * s[:, 1:2]).reshape(ch // 8, 8, tn).sum(axis=0)
        return a0 + p0, a1 + p1

    z = jnp.zeros((8, tn), F32)
    a0, a1 = lax.fori_loop(0, d // ch, body, (z, z))
    o_ref[0, 0:1, :] = a0.sum(axis=0, keepdims=True) + b_ref[0]
    o_ref[0, 1:2, :] = a1.sum(axis=0, keepdims=True) + b_ref[0]


def _adaln(s_cols, ada_w, ada_b):
    depth, d, n6 = ada_w.shape
    tn = _tile(n6, 512, 128)
    return pl.pallas_call(
        _adaln_kernel,
        grid=(depth, n6 // tn),
        in_specs=[
            pl.BlockSpec((d, 2), lambda l, j: (0, 0)),
            pl.BlockSpec((1, d, tn), lambda l, j: (l, 0, j)),
            pl.BlockSpec((1, 1, tn), lambda l, j: (l, 0, j)),
        ],
        out_specs=pl.BlockSpec((1, 2, tn), lambda l, j: (l, 0, j)),
        out_shape=jax.ShapeDtypeStruct((depth, 2, n6), F32),
        compiler_params=_params(("arbitrary", "arbitrary")),
        name="adaln",
    )(s_cols, ada_w, ada_b.reshape(depth, 1, n6))


def _normmod_kernel(x_ref, g_ref, sc_ref, sh_ref, *o_refs):
    x = x_ref[...]
    ms = jnp.mean(x * x, axis=-1, keepdims=True)
    y = x * lax.rsqrt(ms + EPS) * g_ref[...]
    h = y * (1.0 + sc_ref[...]) + sh_ref[...]
    o_refs[0][...] = h.astype(BF16)
    if len(o_refs) > 1:
        o_refs[1][...] = _pack_halves(h)


def _normmod(x, g, sc, sh, packed=False):
    m, d = x.shape
    tm = _tile(m, 256)
    vec = pl.BlockSpec((1, d), lambda i: (0, 0))
    out_shape = [jax.ShapeDtypeStruct((m, d), BF16)]
    out_specs = [pl.BlockSpec((tm, d), lambda i: (i, 0))]
    if packed:
        out_shape.append(jax.ShapeDtypeStruct((m, d // 2), U32))
        out_specs.append(pl.BlockSpec((tm, d // 2), lambda i: (i, 0)))
    res = pl.pallas_call(
        _normmod_kernel,
        grid=(m // tm,),
        in_specs=[pl.BlockSpec((tm, d), lambda i: (i, 0)), vec, vec, vec],
        out_specs=out_specs,
        out_shape=out_shape,
        compiler_params=_params(("arbitrary",)),
        name="normmod",
    )(x, g.reshape(1, d), sc.reshape(1, d), sh.reshape(1, d))
    return res if packed else res[0]


def _mm_kernel(a_ref, b_ref, o_ref):
    o_ref[...] = jnp.dot(a_ref[...], b_ref[...], preferred_element_type=F32).astype(o_ref.dtype)


def _matmul(a, b, out_dtype, col_block_off=0, n_out=None, tm_pref=1024, tn_pref=1024):
    m, k = a.shape
    n_out = b.shape[1] if n_out is None else n_out
    tm = _tile(m, tm_pref)
    tn = _tile(n_out, tn_pref, 128)
    return pl.pallas_call(
        _mm_kernel,
        grid=(m // tm, n_out // tn),
        in_specs=[
            pl.BlockSpec((tm, k), lambda i, j: (i, 0)),
            pl.BlockSpec((k, tn), lambda i, j: (0, j + col_block_off)),
        ],
        out_specs=pl.BlockSpec((tm, tn), lambda i, j: (i, j)),
        out_shape=jax.ShapeDtypeStruct((m, n_out), out_dtype),
        compiler_params=_params(("arbitrary", "arbitrary")),
        name="matmul",
    )(a, b)


def _head_norm_rope(x, g, c, se, so):
    ms = jnp.mean(x * x, axis=-1, keepdims=True)
    y = x * lax.rsqrt(ms + EPS) * g
    return y * c + pltpu.roll(y, HEAD_DIM - 1, 1) * se + pltpu.roll(y, 1, 1) * so


def _qprep_kernel(p_ref, c_ref, se_ref, so_ref, g_ref, o_ref, *, nheads, scale):
    c, se, so, g = c_ref[...], se_ref[...], so_ref[...], g_ref[...]
    for h in range(nheads):
        sl = slice(h * HEAD_DIM, (h + 1) * HEAD_DIM)
        r = _head_norm_rope(p_ref[:, sl].astype(F32), g, c, se, so) * scale
        o_ref[sl, :] = r.T.astype(o_ref.dtype)


def _qprep(p, tabs, g):
    m = p.shape[0]
    tq = _tile(m, 512, 128)
    tab = pl.BlockSpec((tq, HEAD_DIM), lambda i: (i, 0))
    return pl.pallas_call(
        functools.partial(_qprep_kernel, nheads=GQA_HEADS, scale=HEAD_DIM ** -0.5 * LOG2E),
        grid=(m // tq,),
        in_specs=[pl.BlockSpec((tq, GQA_WIDTH), lambda i: (i, QA_OFF // GQA_WIDTH)), tab, tab, tab,
                  pl.BlockSpec((1, HEAD_DIM), lambda i: (0, 0))],
        out_specs=pl.BlockSpec((GQA_WIDTH, tq), lambda i: (0, i)),
        out_shape=jax.ShapeDtypeStruct((GQA_WIDTH, m), BF16),
        compiler_params=_params(("arbitrary",)),
        name="qprep",
    )(p, *tabs, g.reshape(1, HEAD_DIM))


VT_ROWS = HEAD_DIM + 16


def _kvprep_kernel(pc_ref, pl_ref, c_ref, se_ref, so_ref, g_ref, k_ref, vt_ref, *, n_ctx_blocks):
    i = pl.program_id(0)
    c, se, so, g = c_ref[...], se_ref[...], so_ref[...], g_ref[...]
    t = k_ref.shape[0]
    ones_row = jnp.where(lax.broadcasted_iota(I32, (VT_ROWS - HEAD_DIM, t), 0) == 0, 1.0, 0.0).astype(BF16)

    def run(p_ref):
        for h in range(GQA_KV_HEADS):
            sl = slice(h * HEAD_DIM, (h + 1) * HEAD_DIM)
            k_ref[:, sl] = _head_norm_rope(p_ref[:, sl].astype(F32), g, c, se, so).astype(k_ref.dtype)
            v = p_ref[:, GQA_KV_WIDTH + h * HEAD_DIM:GQA_KV_WIDTH + (h + 1) * HEAD_DIM].astype(F32)
            vt_ref[h * VT_ROWS:h * VT_ROWS + HEAD_DIM, :] = v.T.astype(vt_ref.dtype)
            vt_ref[h * VT_ROWS + HEAD_DIM:(h + 1) * VT_ROWS, :] = ones_row

    @pl.when(i < n_ctx_blocks)
    def _():
        run(pc_ref)

    @pl.when(i >= n_ctx_blocks)
    def _():
        run(pl_ref)


def _kvprep(p_ctx, p_lat, tabs, g):
    l, n = p_ctx.shape[0], p_lat.shape[0]
    t = _tile(math.gcd(l, n), 256, 128)
    ncb = l // t
    w = 2 * GQA_KV_WIDTH
    cb = KA_OFF // w
    tab = pl.BlockSpec((t, HEAD_DIM), lambda i: (i, 0))
    return pl.pallas_call(
        functools.partial(_kvprep_kernel, n_ctx_blocks=ncb),
        grid=((l + n) // t,),
        in_specs=[pl.BlockSpec((t, w), lambda i: (jnp.minimum(i, ncb - 1), cb)),
                  pl.BlockSpec((t, w), lambda i: (jnp.maximum(i - ncb, 0), cb)),
                  tab, tab, tab, pl.BlockSpec((1, HEAD_DIM), lambda i: (0, 0))],
        out_specs=[pl.BlockSpec((t, GQA_KV_WIDTH), lambda i: (i, 0)),
                   pl.BlockSpec((GQA_KV_HEADS * VT_ROWS, t), lambda i: (0, i))],
        out_shape=[jax.ShapeDtypeStruct((l + n, GQA_KV_WIDTH), BF16),
                   jax.ShapeDtypeStruct((GQA_KV_HEADS * VT_ROWS, l + n), BF16)],
        compiler_params=_params(("arbitrary",)),
        name="kvprep",
    )(p_ctx, p_lat, *tabs, g.reshape(1, HEAD_DIM))


def _gqa_flash_kernel(qT_ref, k_ref, vT_ref, o_ref, m_ref, acc_ref, *, group, cb):
    j = pl.program_id(2)
    tq = qT_ref.shape[1]

    @pl.when(j == 0)
    def _():
        m_ref[...] = jnp.full(m_ref.shape, -jnp.inf, F32)
        acc_ref[...] = jnp.zeros(acc_ref.shape, F32)

    k = k_ref[...]
    vT = vT_ref[...]
    nsub = tq // cb
    nblk = group * nsub
    st = {}

    def scores(c):
        h, sub = divmod(c, nsub)
        cols = slice(c * cb, (c + 1) * cb)
        qT = qT_ref[h * HEAD_DIM:(h + 1) * HEAD_DIM, sub * cb:(sub + 1) * cb]
        sT = jnp.dot(k, qT, preferred_element_type=F32)
        m_prev = m_ref[:, cols]
        m_new = jnp.maximum(m_prev, jnp.max(sT, axis=0, keepdims=True))
        m_ref[:, cols] = m_new
        st[c] = (sT, m_new, jnp.exp2(m_prev - m_new))

    def probs(c):
        sT, m_new, alpha = st[c]
        st[c] = (jnp.exp2(sT - m_new).astype(BF16), alpha)

    def accumulate(c):
        cols = slice(c * cb, (c + 1) * cb)
        pT, alpha = st.pop(c)
        acc_ref[:, cols] = alpha * acc_ref[:, cols] + jnp.dot(vT, pT, preferred_element_type=F32)

    scores(0)
    for c in range(nblk):
        if c + 1 < nblk:
            scores(c + 1)
        if c >= 1:
            accumulate(c - 1)
        probs(c)
    accumulate(nblk - 1)

    @pl.when(j == pl.num_programs(2) - 1)
    def _():
        for h in range(group):
            cols = slice(h * tq, (h + 1) * tq)
            o = acc_ref[:HEAD_DIM, cols] * (1.0 / acc_ref[HEAD_DIM:HEAD_DIM + 1, cols])
            o_ref[:, h * HEAD_DIM:(h + 1) * HEAD_DIM] = o.T.astype(o_ref.dtype)


def _gqa_flash(qT, k, vT, tq_pref=512, tk_pref=1280, cb=256):
    nq, tk_total = qT.shape[1], k.shape[0]
    group = GQA_HEADS // GQA_KV_HEADS
    tq = _tile(nq, tq_pref, 128)
    tk = _tile(tk_total, tk_pref, 128)
    gw = group * HEAD_DIM
    return pl.pallas_call(
        functools.partial(_gqa_flash_kernel, group=group, cb=min(cb, tq)),
        grid=(GQA_KV_HEADS, nq // tq, tk_total // tk),
        in_specs=[
            pl.BlockSpec((gw, tq), lambda g, i, j: (g, i)),
            pl.BlockSpec((tk, HEAD_DIM), lambda g, i, j: (j, g)),
            pl.BlockSpec((VT_ROWS, tk), lambda g, i, j: (g, j)),
        ],
        out_specs=pl.BlockSpec((tq, gw), lambda g, i, j: (i, g)),
        out_shape=jax.ShapeDtypeStruct((nq, GQA_WIDTH), BF16),
        scratch_shapes=[
            pltpu.VMEM((1, group * tq), F32),
            pltpu.VMEM((VT_ROWS, group * tq), F32),
        ],
        compiler_params=_params(("arbitrary", "arbitrary", "arbitrary")),
        name="gqa_flash",
    )(qT, k, vT)


def _flash_kernel(q_ref, k_ref, v_ref, o_ref, qs_ref, m_ref, l_ref, acc_ref, *, group, scale):
    j = pl.program_id(2)
    tq = q_ref.shape[0]

    @pl.when(j == 0)
    def _():
        for h in range(group):
            sl = slice(h * HEAD_DIM, (h + 1) * HEAD_DIM)
            qs_ref[h * tq:(h + 1) * tq, :] = (q_ref[:, sl].astype(F32) * scale).astype(BF16)
        m_ref[...] = jnp.full(m_ref.shape, -jnp.inf, F32)
        l_ref[...] = jnp.zeros(l_ref.shape, F32)
        acc_ref[...] = jnp.zeros(acc_ref.shape, F32)

    s = lax.dot_general(qs_ref[...], k_ref[...], (((1,), (1,)), ((), ())), preferred_element_type=F32)
    m_prev = m_ref[...]
    m_new = jnp.maximum(m_prev, jnp.max(s, axis=-1, keepdims=True))
    alpha = jnp.exp2(m_prev - m_new)
    p = jnp.exp2(s - m_new)
    l_ref[...] = alpha * l_ref[...] + jnp.sum(p, axis=-1, keepdims=True)
    acc_ref[...] = alpha * acc_ref[...] + jnp.dot(p.astype(BF16), v_ref[...], preferred_element_type=F32)
    m_ref[...] = m_new

    @pl.when(j == pl.num_programs(2) - 1)
    def _():
        inv = 1.0 / l_ref[...]
        for h in range(group):
            sl = slice(h * HEAD_DIM, (h + 1) * HEAD_DIM)
            o_ref[:, sl] = (acc_ref[h * tq:(h + 1) * tq, :] * inv[h * tq:(h + 1) * tq]).astype(o_ref.dtype)


def _flash(q, q_off, k, k_off, v, v_off, n_kv_heads, group, tq_pref=512, tk_pref=1280):
    nq, tk_total = q.shape[0], k.shape[0]
    tq = _tile(nq, tq_pref)
    tk = _tile(tk_total, tk_pref, 128)
    gw = group * HEAD_DIM
    scale = HEAD_DIM ** -0.5 * LOG2E
    return pl.pallas_call(
        functools.partial(_flash_kernel, group=group, scale=scale),
        grid=(n_kv_heads, nq // tq, tk_total // tk),
        in_specs=[
            pl.BlockSpec((tq, gw), lambda g, i, j: (i, q_off // gw + g)),
            pl.BlockSpec((tk, HEAD_DIM), lambda g, i, j: (j, k_off // HEAD_DIM + g)),
            pl.BlockSpec((tk, HEAD_DIM), lambda g, i, j: (j, v_off // HEAD_DIM + g)),
        ],
        out_specs=pl.BlockSpec((tq, gw), lambda g, i, j: (i, g)),
        out_shape=jax.ShapeDtypeStruct((nq, n_kv_heads * gw), BF16),
        scratch_shapes=[
            pltpu.VMEM((group * tq, HEAD_DIM), BF16),
            pltpu.VMEM((group * tq, 1), F32),
            pltpu.VMEM((group * tq, 1), F32),
            pltpu.VMEM((group * tq, HEAD_DIM), F32),
        ],
        compiler_params=_params(("arbitrary", "arbitrary", "arbitrary")),
        name="flash",
    )(q, k, v)


NA_QROWS = 8
NA_BAND = 16


def _na_kernel(q_ref, k0, k1, k2, k3, v0, v1, v2, v3, kc_ref, vc_ref, tb_ref, o_ref, kb_ref, vb_ref,
               *, rows):
    b = pl.program_id(0)
    quarter = NA_BAND * GRID_W // 4
    for d, (kr, vr) in enumerate(((k0, v0), (k1, v1), (k2, v2), (k3, v3))):
        kb_ref[d * quarter:(d + 1) * quarter, :] = kr[...]
        vb_ref[d * quarter:(d + 1) * quarter, :] = vr[...]
    band_start = jnp.clip(NA_QROWS * b - NA_WIN_ROWS // 2, 0, rows - NA_BAND)
    scale = HEAD_DIM ** -0.5
    win = NA_WIN_ROWS * GRID_W

    def body(a, carry):
        r = NA_QROWS * b + a
        r_start = jnp.clip(r - NA_WIN_ROWS // 2, 0, rows - NA_WIN_ROWS)
        dd = r - r_start
        koff = pl.multiple_of((r_start - band_start) * GRID_W, GRID_W)
        qoff = pl.multiple_of(a * GRID_W, GRID_W)
        q_a = q_ref[pl.ds(qoff, GRID_W), :]
        kw = kb_ref[pl.ds(koff, win), :]
        vw = vb_ref[pl.ds(koff, win), :]
        for h in range(NA_HEADS):
            sl = slice(h * HEAD_DIM, (h + 1) * HEAD_DIM)
            qh = q_a[:, sl]
            dn = (((1,), (1,)), ((), ()))
            s_nb = lax.dot_general(qh, kw[:, sl], dn, preferred_element_type=F32) * scale + tb_ref[h, dd]
            s_c = lax.dot_general(qh, kc_ref[:, sl], dn, preferred_element_type=F32) * scale
            m = jnp.maximum(jnp.max(s_nb, axis=-1, keepdims=True), jnp.max(s_c, axis=-1, keepdims=True))
            p_nb = jnp.exp(s_nb - m)
            p_c = jnp.exp(s_c - m)
            l = jnp.sum(p_nb, axis=-1, keepdims=True) + jnp.sum(p_c, axis=-1, keepdims=True)
            o = (jnp.dot(p_c.astype(BF16), vc_ref[:, sl], preferred_element_type=F32)
                 + jnp.dot(p_nb.astype(BF16), vw[:, sl], preferred_element_type=F32))
            o_ref[pl.ds(qoff, GRID_W), sl] = (o / l).astype(o_ref.dtype)
        return carry

    lax.fori_loop(0, NA_QROWS, body, 0)


def _na_bias_table(rel_bias):
    qc = jnp.arange(GRID_W, dtype=I32)
    kc = jnp.arange(GRID_W, dtype=I32)
    col_start = jnp.clip(qc - NA_WIN_COLS // 2, 0, GRID_W - NA_WIN_COLS)
    valid = (kc[None, :] >= col_start[:, None]) & (kc[None, :] < col_start[:, None] + NA_WIN_COLS)
    coff = jnp.clip(kc[None, :] - qc[:, None] + NA_WIN_COLS - 1, 0, 2 * NA_WIN_COLS - 2)
    i = jnp.arange(NA_WIN_ROWS, dtype=I32)
    dd = jnp.arange(NA_WIN_ROWS, dtype=I32)
    ro = i[None, :] + NA_WIN_ROWS - 1 - dd[:, None]
    tb = rel_bias[:, ro[:, :, None, None], coff[None, None, :, :]].astype(F32)
    tb = jnp.where(valid[None, None, None], tb, MASK_VALUE)
    return tb.transpose(0, 1, 3, 2, 4).reshape(NA_HEADS, NA_WIN_ROWS, GRID_W, NA_WIN_ROWS * GRID_W)


def _na(p_lat, p_ctx, rel_bias):
    n, l = p_lat.shape[0], p_ctx.shape[0]
    rows = n // GRID_W
    assert rows % NA_QROWS == 0 and rows >= NA_BAND
    tq = NA_QROWS * GRID_W
    quarter = NA_BAND * GRID_W // 4
    nquart = n // quarter

    def band_block(d, col):
        return pl.BlockSpec((quarter, NA_WIDTH),
                            lambda b: (jnp.clip(2 * b - 1, 0, nquart - 4) + d, col))

    kcol, vcol = KN_OFF // NA_WIDTH, VN_OFF // NA_WIDTH
    tb = _na_bias_table(rel_bias)
    return pl.pallas_call(
        functools.partial(_na_kernel, rows=rows),
        grid=(rows // NA_QROWS,),
        in_specs=[pl.BlockSpec((tq, NA_WIDTH), lambda b: (b, QN_OFF // NA_WIDTH))]
        + [band_block(d, kcol) for d in range(4)]
        + [band_block(d, vcol) for d in range(4)]
        + [pl.BlockSpec((l, NA_WIDTH), lambda b: (0, kcol)),
           pl.BlockSpec((l, NA_WIDTH), lambda b: (0, vcol)),
           pl.BlockSpec(tb.shape, lambda b: (0, 0, 0, 0))],
        out_specs=pl.BlockSpec((tq, NA_WIDTH), lambda b: (b, 0)),
        out_shape=jax.ShapeDtypeStruct((n, NA_WIDTH), BF16),
        scratch_shapes=[pltpu.VMEM((NA_BAND * GRID_W, NA_WIDTH), BF16),
                        pltpu.VMEM((NA_BAND * GRID_W, NA_WIDTH), BF16)],
        compiler_params=_params(("arbitrary",)),
        name="natten",
    )(p_lat, *([p_lat] * 8), p_ctx, p_ctx, tb)


def _dft_cs(n):
    k = np.arange(n, dtype=np.float64)
    ang = 2.0 * np.pi * np.outer(k, k) / n
    return np.cos(ang), np.sin(ang)


def _f1_kernel(x_ref, da_ref, db_ref, twr_ref, twi_ref, u_ref, *, tb, n_hi):
    w = FOURIER_WIDTH
    for j in range(tb):
        x = x_ref[:, j, :].astype(BF16)
        zs = [jnp.dot(x[:, g * HEAD_DIM:(g + 1) * HEAD_DIM], da_ref[...], preferred_element_type=F32)
              for g in range(FOURIER_GROUPS)]
        zr = jnp.concatenate([z[:, :HEAD_DIM] for z in zs], axis=1)
        zi = jnp.concatenate([z[:, HEAD_DIM:] for z in zs], axis=1)
        zst = jnp.concatenate([zr, zi], axis=0).astype(BF16)
        u = jnp.dot(db_ref[...], zst, preferred_element_type=F32)
        ur, ui = u[:n_hi], u[n_hi:]
        tr, ti = twr_ref[j], twi_ref[j]
        u_ref[:, j, :w] = ur * tr - ui * ti
        u_ref[:, j, w:] = ur * ti + ui * tr


def _f2_kernel(u_ref, dc_ref, y_ref, *, ta):
    w = FOURIER_WIDTH
    for j in range(ta):
        u = u_ref[j]
        ust = jnp.concatenate([u[:, :w], u[:, w:]], axis=0).astype(BF16)
        y_ref[:, j, :] = jnp.dot(dc_ref[...], ust, preferred_element_type=F32)


def _fourier(xf):
    n, w = xf.shape
    n_lo = 128
    n_hi = n // n_lo
    assert n_hi * n_lo == n and n_hi % 8 == 0
    cc, sc = _dft_cs(HEAD_DIM)
    da = jnp.asarray(np.concatenate([cc, -sc], axis=1), BF16)
    cn, sn = _dft_cs(n_hi)
    db = jnp.asarray(np.block([[cn, sn], [-sn, cn]]), BF16)
    ang = 2.0 * np.pi * np.outer(np.arange(n_lo), np.arange(n_hi)) / n
    twr = jnp.asarray(np.cos(ang)[:, :, None], F32)
    twi = jnp.asarray(-np.sin(ang)[:, :, None], F32)
    cl, sl = _dft_cs(n_lo)
    dc = jnp.asarray(np.concatenate([cl, sl], axis=1), BF16)
    tb = 8
    u = pl.pallas_call(
        functools.partial(_f1_kernel, tb=tb, n_hi=n_hi),
        grid=(n_lo // tb,),
        in_specs=[pl.BlockSpec((n_hi, tb, w), lambda j: (0, j, 0)),
                  pl.BlockSpec(da.shape, lambda j: (0, 0)),
                  pl.BlockSpec(db.shape, lambda j: (0, 0)),
                  pl.BlockSpec((tb, n_hi, 1), lambda j: (j, 0, 0)),
                  pl.BlockSpec((tb, n_hi, 1), lambda j: (j, 0, 0))],
        out_specs=pl.BlockSpec((n_hi, tb, 2 * w), lambda j: (0, j, 0)),
        out_shape=jax.ShapeDtypeStruct((n_hi, n_lo, 2 * w), F32),
        compiler_params=_params(("arbitrary",)),
        name="fourier_stage1",
    )(xf.reshape(n_hi, n_lo, w), da, db, twr, twi)
    ta = 8
    y = pl.pallas_call(
        functools.partial(_f2_kernel, ta=ta),
        grid=(n_hi // ta,),
        in_specs=[pl.BlockSpec((ta, n_lo, 2 * w), lambda i: (i, 0, 0)),
                  pl.BlockSpec(dc.shape, lambda i: (0, 0))],
        out_specs=pl.BlockSpec((n_lo, ta, w), lambda i: (0, i, 0)),
        out_shape=jax.ShapeDtypeStruct((n_lo, n_hi, w), F32),
        compiler_params=_params(("arbitrary",)),
        name="fourier_stage2",
    )(u, dc)
    return y.reshape(n, w)


def _fourier_small_kernel(x_ref, dch_ref, dpos_ref, y_ref):
    x = x_ref[...].astype(BF16)
    for g in range(FOURIER_GROUPS):
        sl = slice(g * HEAD_DIM, (g + 1) * HEAD_DIM)
        a = jnp.dot(x[:, sl], dch_ref[...], preferred_element_type=F32)
        ast = jnp.concatenate([a[:, :HEAD_DIM], a[:, HEAD_DIM:]], axis=0).astype(BF16)
        y_ref[:, sl] = jnp.dot(dpos_ref[...], ast, preferred_element_type=F32)


def _fourier_small(xf):
    t, w = xf.shape
    cc, sc = _dft_cs(HEAD_DIM)
    dch = jnp.asarray(np.concatenate([cc, sc], axis=1), BF16)
    cp, sp = _dft_cs(t)
    dpos = jnp.asarray(np.concatenate([cp, -sp], axis=1), BF16)
    return pl.pallas_call(
        _fourier_small_kernel,
        grid=(1,),
        in_specs=[pl.BlockSpec((t, w), lambda i: (0, 0)),
                  pl.BlockSpec(dch.shape, lambda i: (0, 0)),
                  pl.BlockSpec(dpos.shape, lambda i: (0, 0))],
        out_specs=pl.BlockSpec((t, w), lambda i: (0, 0)),
        out_shape=jax.ShapeDtypeStruct((t, w), F32),
        compiler_params=_params(("arbitrary",)),
        name="fourier_small",
    )(xf, dch, dpos)


def _outproj_kernel(og_ref, on_ref, of_ref, g_ref, w_ref, x_ref, gate_ref, o_ref, m_ref):
    @pl.when(pl.program_id(1) == 0)
    def _():
        def nrm(v, g):
            vf = v.astype(F32)
            return (vf * lax.rsqrt(jnp.mean(vf * vf, axis=-1, keepdims=True) + EPS) * g).astype(BF16)

        a, b = GQA_WIDTH, GQA_WIDTH + NA_WIDTH
        m_ref[:, :a] = nrm(og_ref[...], g_ref[:, :a])
        m_ref[:, a:b] = nrm(on_ref[...], g_ref[:, a:b])
        m_ref[:, b:] = nrm(of_ref[...], g_ref[:, b:])

    acc = jnp.dot(m_ref[...], w_ref[...], preferred_element_type=F32)
    o_ref[...] = x_ref[...] + gate_ref[...] * acc


def _outproj(o_gqa, o_na, o_four, g_out, w_out, x, gate):
    m, d = x.shape
    tm = _tile(m, 512)
    tn = _tile(d, 1024, 128)
    return pl.pallas_call(
        _outproj_kernel,
        grid=(m // tm, d // tn),
        in_specs=[pl.BlockSpec((tm, GQA_WIDTH), lambda i, j: (i, 0)),
                  pl.BlockSpec((tm, NA_WIDTH), lambda i, j: (i, 0)),
                  pl.BlockSpec((tm, FOURIER_WIDTH), lambda i, j: (i, 0)),
                  pl.BlockSpec((1, MIX_WIDTH), lambda i, j: (0, 0)),
                  pl.BlockSpec((MIX_WIDTH, tn), lambda i, j: (0, j)),
                  pl.BlockSpec((tm, tn), lambda i, j: (i, j)),
                  pl.BlockSpec((1, tn), lambda i, j: (0, j))],
        out_specs=pl.BlockSpec((tm, tn), lambda i, j: (i, j)),
        out_shape=jax.ShapeDtypeStruct((m, d), F32),
        scratch_shapes=[pltpu.VMEM((tm, MIX_WIDTH), BF16)],
        compiler_params=_params(("arbitrary", "arbitrary")),
        name="outproj",
    )(o_gqa, o_na, o_four, g_out.reshape(1, MIX_WIDTH), w_out, x, gate.reshape(1, d))


def _router_kernel(h_ref, rw_ref, rb_ref, idx_ref, wt_ref, rank_ref, cnt_ref, carry_ref):
    step = pl.program_id(0)

    @pl.when(step == 0)
    def _():
        carry_ref[...] = jnp.zeros(carry_ref.shape, F32)

    tm = h_ref.shape[0]
    e = N_EXPERTS
    gsz = e // N_EXPERT_GROUPS
    neg = -jnp.inf
    logits = jnp.dot(h_ref[...], rw_ref[...], preferred_element_type=F32)
    scores = _sigmoid(logits)
    biased = scores + rb_ref[...]
    lane = lax.broadcasted_iota(I32, (tm, e), 1).astype(F32)
    lane_grp = lax.broadcasted_iota(I32, (tm, e), 1) // gsz

    def first_argmax(v):
        m = jnp.max(v, axis=-1, keepdims=True)
        return m, jnp.min(jnp.where(v == m, lane, float(e)), axis=-1, keepdims=True)

    gs = []
    for g in range(N_EXPERT_GROUPS):
        vg = jnp.where(lane_grp == g, biased, neg)
        m1, i1 = first_argmax(vg)
        m2 = jnp.max(jnp.where(lane == i1, neg, vg), axis=-1, keepdims=True)
        gs.append(m1 + m2)
    masked = jnp.full((tm, e), neg, F32)
    for g in range(N_EXPERT_GROUPS):
        ahead = jnp.zeros((tm, 1), F32)
        for g2 in range(N_EXPERT_GROUPS):
            if g2 == g:
                continue
            better = (gs[g2] > gs[g]) | ((gs[g2] == gs[g]) & (g2 < g))
            ahead = ahead + jnp.where(better, 1.0, 0.0)
        masked = jnp.where((lane_grp == g) & (ahead < TOPK_GROUPS), biased, masked)
    sel = jnp.zeros((tm, e), F32)
    idxs, ws = [], []
    cur = masked
    for _ in range(TOP_K):
        _, ik = first_argmax(cur)
        hit = lane == ik
        ws.append(jnp.sum(jnp.where(hit, scores, 0.0), axis=-1, keepdims=True))
        cur = jnp.where(hit, neg, cur)
        sel = sel + jnp.where(hit, 1.0, 0.0)
        idxs.append(ik)
    wsum = ws[0]
    for wk in ws[1:]:
        wsum = wsum + wk
    rr = lax.broadcasted_iota(I32, (tm, tm), 0)
    cc = lax.broadcasted_iota(I32, (tm, tm), 1)
    tri = jnp.where(cc < rr, 1.0, 0.0).astype(BF16)
    pref = jnp.dot(tri, sel.astype(BF16), preferred_element_type=F32) + carry_ref[...]
    ranks = [jnp.sum(jnp.where(lane == ik, pref, 0.0), axis=-1, keepdims=True) for ik in idxs]
    carry_ref[...] = carry_ref[...] + jnp.sum(sel, axis=0, keepdims=True)

    lane_o = lax.broadcasted_iota(I32, (tm, 128), 1)
    o_idx = jnp.zeros((tm, 128), F32)
    o_w = jnp.zeros((tm, 128), F32)
    o_rank = jnp.zeros((tm, 128), F32)
    for k in range(TOP_K):
        o_idx = jnp.where(lane_o == k, idxs[k], o_idx)
        o_w = jnp.where(lane_o == k, ws[k] / wsum * ROUTED_SCALE, o_w)
        o_rank = jnp.where(lane_o == k, ranks[k], o_rank)
    idx_ref[...] = o_idx.astype(I32)
    wt_ref[...] = o_w
    rank_ref[...] = o_rank.astype(I32)
    cnt_ref[...] = jnp.broadcast_to(carry_ref[...], cnt_ref.shape)


def _router(h, rw, rb):
    t, d = h.shape
    tm = _tile(t, 256)
    blk = pl.BlockSpec((tm, 128), lambda i: (i, 0))
    idx, wt, rank, cnt = pl.pallas_call(
        _router_kernel,
        grid=(t // tm,),
        in_specs=[pl.BlockSpec((tm, d), lambda i: (i, 0)),
                  pl.BlockSpec((d, N_EXPERTS), lambda i: (0, 0)),
                  pl.BlockSpec((1, N_EXPERTS), lambda i: (0, 0))],
        out_specs=[blk, blk, blk, pl.BlockSpec((8, N_EXPERTS), lambda i: (0, 0))],
        out_shape=[jax.ShapeDtypeStruct((t, 128), I32), jax.ShapeDtypeStruct((t, 128), F32),
                   jax.ShapeDtypeStruct((t, 128), I32), jax.ShapeDtypeStruct((8, N_EXPERTS), F32)],
        scratch_shapes=[pltpu.VMEM((1, N_EXPERTS), F32)],
        compiler_params=_params(("arbitrary",)),
        name="router",
    )(h, rw, rb.reshape(1, N_EXPERTS))
    return idx[:, :TOP_K], wt[:, :TOP_K], rank[:, :TOP_K], cnt[0].astype(I32)


def _gather_pipeline(step, n_used, idx_hbm, idx_smem, idx_sem, src_hbm, bufs, buf_sem, issue_rows):
    slot = step % 2
    nslot = 1 - slot

    def idx_copy(s, sl):
        return pltpu.make_async_copy(idx_hbm.at[s], idx_smem.at[sl], idx_sem.at[sl])

    def buf_wait(sl):
        pltpu.make_async_copy(bufs.at[sl], bufs.at[sl], buf_sem.at[sl]).wait()

    @pl.when(step == 0)
    def _():
        idx_copy(0, 0).start()
        idx_copy(0, 0).wait()
        issue_rows(0, lambda r: idx_smem[0, r])

        @pl.when(n_used > 1)
        def _():
            idx_copy(1, 1).start()

    @pl.when(step + 1 < n_used)
    def _():
        idx_copy(step + 1, nslot).wait()
        issue_rows(nslot, lambda r: idx_smem[nslot, r])

    @pl.when(step + 2 < n_used)
    def _():
        idx_copy(step + 2, slot).start()

    buf_wait(slot)
    return slot


def _expert_kernel(be_ref, nu_ref, rt_hbm, hp_hbm, wgu_ref, wd_ref, o_ref, idx_smem, xbuf, idx_sem, x_sem):
    b = pl.program_id(0)
    n_used = nu_ref[0]
    bm = xbuf.shape[1]
    dh = xbuf.shape[2]
    ff = wd_ref.shape[1]

    @pl.when(b < n_used)
    def _():
        def issue_rows(slot, read_index):
            def body(r, carry):
                tok = read_index(r)
                pltpu.make_async_copy(hp_hbm.at[pl.ds(tok, 1), :], xbuf.at[slot, pl.ds(r, 1), :],
                                      x_sem.at[slot]).start()
                return carry

            lax.fori_loop(0, bm, body, 0)

        slot = _gather_pipeline(b, n_used, rt_hbm, idx_smem, idx_sem, hp_hbm, xbuf, x_sem, issue_rows)
        lo, hi = _unpack_halves(xbuf[slot])
        h = (jnp.dot(lo.astype(BF16), wgu_ref[0, :dh, :], preferred_element_type=F32)
             + jnp.dot(hi.astype(BF16), wgu_ref[0, dh:, :], preferred_element_type=F32))
        gate, up = h[:, :ff], h[:, ff:]
        act = (gate * _sigmoid(gate) * up).astype(BF16)
        y = jnp.dot(act, wd_ref[0], preferred_element_type=F32)
        o_ref[...] = _pack_halves(y)

    @pl.when(b >= n_used)
    def _():
        o_ref[...] = jnp.zeros(o_ref.shape, o_ref.dtype)


def _experts(hp, row_tok, block_e, n_used, wgu, wd):
    n_blocks, bm = row_tok.shape
    dh = hp.shape[1]
    d = 2 * dh
    ff = wd.shape[1]
    grid_spec = pltpu.PrefetchScalarGridSpec(
        num_scalar_prefetch=2,
        grid=(n_blocks,),
        in_specs=[pl.BlockSpec(memory_space=pl.ANY),
                  pl.BlockSpec(memory_space=pl.ANY),
                  pl.BlockSpec((1, d, 2 * ff), lambda b, be, nu: (be[b], 0, 0)),
                  pl.BlockSpec((1, ff, d), lambda b, be, nu: (be[b], 0, 0))],
        out_specs=pl.BlockSpec((bm, dh), lambda b, be, nu: (b, 0)),
        scratch_shapes=[pltpu.SMEM((2, bm), I32),
                        pltpu.VMEM((2, bm, dh), U32),
                        pltpu.SemaphoreType.DMA((2,)),
                        pltpu.SemaphoreType.DMA((2,))],
    )
    return pl.pallas_call(
        _expert_kernel,
        grid_spec=grid_spec,
        out_shape=jax.ShapeDtypeStruct((n_blocks * bm, dh), U32),
        compiler_params=_params(("arbitrary",)),
        name="experts",
    )(block_e, n_used, row_tok, hp, wgu, wd)


def _shared_kernel(h_ref, wgu_ref, wd_ref, o_ref):
    ff = wd_ref.shape[0]
    h = jnp.dot(h_ref[...], wgu_ref[...], preferred_element_type=F32)
    gate, up = h[:, :ff], h[:, ff:]
    act = (gate * _sigmoid(gate) * up).astype(BF16)
    o_ref[...] = jnp.dot(act, wd_ref[...], preferred_element_type=F32).astype(o_ref.dtype)


def _shared(h, wgu, wd):
    t, d = h.shape
    ff = wd.shape[0]
    tm = _tile(t, 512)
    return pl.pallas_call(
        _shared_kernel,
        grid=(t // tm,),
        in_specs=[pl.BlockSpec((tm, d), lambda i: (i, 0)),
                  pl.BlockSpec((d, 2 * ff), lambda i: (0, 0)),
                  pl.BlockSpec((ff, d), lambda i: (0, 0))],
        out_specs=pl.BlockSpec((tm, d), lambda i: (i, 0)),
        out_shape=jax.ShapeDtypeStruct((t, d), BF16),
        compiler_params=_params(("arbitrary",)),
        name="shared_expert",
    )(h, wgu, wd)


def _combine_kernel(dest_hbm, ys_hbm, w_ref, ysh_ref, x_ref, gate_ref, *rest, final):
    if final:
        fg_ref, o_ref, idx_smem, gbuf, idx_sem, g_sem = rest
    else:
        o_ref, idx_smem, gbuf, idx_sem, g_sem = rest
    s = pl.program_id(0)
    n_steps = pl.num_programs(0)
    tt = gbuf.shape[2]
    dh = gbuf.shape[3]

    def issue_rows(slot, read_index):
        def body(t, carry):
            for k in range(TOP_K):
                row = read_index(t * TOP_K + k)
                pltpu.make_async_copy(ys_hbm.at[pl.ds(row, 1), :], gbuf.at[slot, k, pl.ds(t, 1), :],
                                      g_sem.at[slot]).start()
            return carry

        lax.fori_loop(0, tt, body, 0)

    slot = _gather_pipeline(s, n_steps, dest_hbm, idx_smem, idx_sem, ys_hbm, gbuf, g_sem, issue_rows)
    w = w_ref[...]
    acc_lo = jnp.zeros((tt, dh), F32)
    acc_hi = jnp.zeros((tt, dh), F32)
    for k in range(TOP_K):
        lo, hi = _unpack_halves(gbuf[slot, k])
        wk = w[:, k:k + 1]
        acc_lo = acc_lo + wk * lo
        acc_hi = acc_hi + wk * hi
    y = jnp.concatenate([acc_lo, acc_hi], axis=1) + ysh_ref[...].astype(F32)
    xn = x_ref[...] + gate_ref[...] * y
    if final:
        xn = xn * lax.rsqrt(jnp.mean(xn * xn, axis=-1, keepdims=True) + EPS) * fg_ref[...]
    o_ref[...] = xn


def _combine(dest, w, ys, ysh, x, gate, final_g=None):
    t, d = x.shape
    tt = _tile(t, COMBINE_TOKENS)
    dh = d // 2
    final = final_g is not None
    in_specs = [pl.BlockSpec(memory_space=pl.ANY),
                pl.BlockSpec(memory_space=pl.ANY),
                pl.BlockSpec((tt, TOP_K), lambda i: (i, 0)),
                pl.BlockSpec((tt, d), lambda i: (i, 0)),
                pl.BlockSpec((tt, d), lambda i: (i, 0)),
                pl.BlockSpec((1, d), lambda i: (0, 0))]
    args = [dest.reshape(t // tt, tt * TOP_K), ys, w, ysh, x, gate.reshape(1, d)]
    if final:
        in_specs.append(pl.BlockSpec((1, d), lambda i: (0, 0)))
        args.append(final_g.reshape(1, d))
    return pl.pallas_call(
        functools.partial(_combine_kernel, final=final),
        grid=(t // tt,),
        in_specs=in_specs,
        out_specs=pl.BlockSpec((tt, d), lambda i: (i, 0)),
        out_shape=jax.ShapeDtypeStruct((t, d), F32),
        scratch_shapes=[pltpu.SMEM((2, tt * TOP_K), I32),
                        pltpu.VMEM((2, TOP_K, tt, dh), U32),
                        pltpu.SemaphoreType.DMA((2,)),
                        pltpu.SemaphoreType.DMA((2,))],
        compiler_params=_params(("arbitrary",)),
        name="combine",
    )(*args)


def _moe_tables(idx, rank, counts, bm):
    t = idx.shape[0]
    e = N_EXPERTS
    tk = t * TOP_K
    padded = (counts + bm - 1) // bm * bm
    pad_end = jnp.cumsum(padded)
    pad_start = pad_end - padded
    onehot = idx[:, :, None] == jnp.arange(e, dtype=I32)[None, None, :]
    dest = jnp.sum(jnp.where(onehot, pad_start[None, None, :], 0), axis=-1) + rank
    n_blocks = -(-tk // bm) + e
    tok = jnp.broadcast_to(jnp.arange(t, dtype=I32)[:, None], (t, TOP_K))
    row_tok = jnp.zeros((n_blocks * bm,), I32).at[dest.reshape(tk)].set(tok.reshape(tk))
    block_first_row = jnp.arange(n_blocks, dtype=I32) * bm
    block_e = jnp.minimum(jnp.sum(pad_end[None, :] <= block_first_row[:, None], axis=1), e - 1).astype(I32)
    n_used = (pad_end[-1] // bm).astype(I32).reshape(1)
    return dest.astype(I32), row_tok.reshape(n_blocks, bm), block_e, n_used


def _rope_tables(n, n_ctx):
    pos = jnp.arange(n, dtype=I32)
    row = (pos // GRID_W).astype(F32)
    col = (pos % GRID_W).astype(F32)
    n_freq = HEAD_DIM // 4
    inv_freq = ROPE_THETA ** (-jnp.arange(n_freq, dtype=F32) / n_freq)
    ang = jnp.concatenate([row[:, None] * inv_freq, col[:, None] * inv_freq], axis=-1)
    cos = jnp.repeat(jnp.cos(ang), 2, axis=1)
    sin = jnp.repeat(jnp.sin(ang), 2, axis=1)
    even = (jnp.arange(HEAD_DIM) % 2 == 0)[None, :]
    c = jnp.concatenate([jnp.ones((n_ctx, HEAD_DIM), F32), cos], axis=0)
    se = jnp.concatenate([jnp.zeros((n_ctx, HEAD_DIM), F32), jnp.where(even, -sin, 0.0)], axis=0)
    so = jnp.concatenate([jnp.zeros((n_ctx, HEAD_DIM), F32), jnp.where(even, 0.0, sin)], axis=0)
    return c, se, so


def kernel(x, c, ctx, c_ctx, ada_w, ada_b, norm1_g, w_in, q_norm_g, k_norm_g, na_rel_bias, out_norm_g,
           w_out, norm2_g, router_w, router_bias, exp_w_gate, exp_w_up, exp_w_down, shared_w_gate,
           shared_w_up, shared_w_down, final_g):
    b, n, d = x.shape
    l = ctx.shape[1]
    depth = ada_w.shape[0]
    assert b == 1
    xl = x[0]
    xc = ctx[0]
    mod = _adaln(jnp.stack([c_ctx, c[0]], axis=1), ada_w, ada_b)
    tabs_all = _rope_tables(n, l)
    tabs_lat = tuple(t[l:] for t in tabs_all)
    tabs_ctx = tuple(t[:l] for t in tabs_all)

    for layer in range(depth):
        last = layer == depth - 1
        sh1, sc1, g1, sh2, sc2, g2 = [mod[layer, 1, i * d:(i + 1) * d] for i in range(6)]
        csh1, csc1, cg1, csh2, csc2, cg2 = [mod[layer, 0, i * d:(i + 1) * d] for i in range(6)]

        w_in_b = w_in[layer].astype(BF16)
        h = _normmod(xl, norm1_g[layer], sc1, sh1)
        hc = _normmod(xc, norm1_g[layer], csc1, csh1)
        p = _matmul(h, w_in_b, BF16, n_out=F_OFF)
        pc = _matmul(hc, w_in_b, BF16, n_out=F_OFF)
        f = _matmul(h, w_in_b, F32, col_block_off=F_OFF // FOURIER_WIDTH, n_out=FOURIER_WIDTH)

        qa_t = _qprep(p, tabs_lat, q_norm_g[layer])
        ka, va_t = _kvprep(pc, p, tabs_all, k_norm_g[layer])
        o_gqa = _gqa_flash(qa_t, ka, va_t)
        o_na = _na(p, pc, na_rel_bias[layer])
        o_four = _fourier(f)
        w_out_b = w_out[layer].astype(BF16)
        xl = _outproj(o_gqa, o_na, o_four, out_norm_g[layer], w_out_b, xl, g1)

        if not last:
            fc = _matmul(hc, w_in_b, F32, col_block_off=F_OFF // FOURIER_WIDTH, n_out=FOURIER_WIDTH)
            qa_ct = _qprep(pc, tabs_ctx, q_norm_g[layer])
            oc_gqa = _gqa_flash(qa_ct, ka[:l], va_t[:, :l])
            oc_na = _flash(pc, QN_OFF, pc, KN_OFF, pc, VN_OFF, NA_HEADS, 1)
            oc_four = _fourier_small(fc)
            xc = _outproj(oc_gqa, oc_na, oc_four, out_norm_g[layer], w_out_b, xc, cg1)

        h2, h2p = _normmod(xl, norm2_g[layer], sc2, sh2, packed=True)
        if not last:
            h2c, h2cp = _normmod(xc, norm2_g[layer], csc2, csh2, packed=True)
            h2 = jnp.concatenate([h2, h2c], axis=0)
            h2p = jnp.concatenate([h2p, h2cp], axis=0)
        idx, wts, rank, counts = _router(h2, router_w[layer].astype(BF16), router_bias[layer])
        dest, row_tok, block_e, n_used = _moe_tables(idx, rank, counts, MOE_ROWS)
        wgu = jnp.concatenate([exp_w_gate[layer], exp_w_up[layer]], axis=-1).astype(BF16)
        ys = _experts(h2p, row_tok, block_e, n_used, wgu, exp_w_down[layer].astype(BF16))
        sgu = jnp.concatenate([shared_w_gate[layer], shared_w_up[layer]], axis=-1).astype(BF16)
        ysh = _shared(h2, sgu, shared_w_down[layer].astype(BF16))
        xl = _combine(dest[:n], wts[:n], ys, ysh[:n], xl, g2, final_g=final_g if last else None)
        if not last:
            xc = _combine(dest[n:], wts[n:], ys, ysh[n:], xc, cg2)
    return xl[None]
```

```python
import functools
import math

import numpy as np
import jax
import jax.numpy as jnp
from jax import lax
from jax.experimental import pallas as pl
from jax.experimental.pallas import tpu as pltpu

F32 = jnp.float32
BF16 = jnp.bfloat16
U32 = jnp.uint32
I32 = jnp.int32

HEAD_DIM = 128
GRID_W = 64
GQA_HEADS = 16
GQA_KV_HEADS = 4
NA_HEADS = 8
NA_WIN_ROWS = 8
NA_WIN_COLS = 16
FOURIER_GROUPS = 8
GQA_WIDTH = GQA_HEADS * HEAD_DIM
GQA_KV_WIDTH = GQA_KV_HEADS * HEAD_DIM
NA_WIDTH = NA_HEADS * HEAD_DIM
FOURIER_WIDTH = FOURIER_GROUPS * HEAD_DIM
MIX_WIDTH = GQA_WIDTH + NA_WIDTH + FOURIER_WIDTH
IN_WIDTH = GQA_WIDTH + 2 * GQA_KV_WIDTH + 3 * NA_WIDTH + FOURIER_WIDTH
QA_OFF = 0
KA_OFF = GQA_WIDTH
VA_OFF = KA_OFF + GQA_KV_WIDTH
QN_OFF = VA_OFF + GQA_KV_WIDTH
KN_OFF = QN_OFF + NA_WIDTH
VN_OFF = KN_OFF + NA_WIDTH
F_OFF = VN_OFF + NA_WIDTH
ROPE_THETA = 10000.0
N_EXPERTS = 64
TOP_K = 8
N_EXPERT_GROUPS = 8
TOPK_GROUPS = 4
ROUTED_SCALE = 2.5
EPS = 1e-6
LOG2E = 1.4426950408889634
MASK_VALUE = -1e30

VMEM_LIMIT = 56 * 1024 * 1024
MOE_ROWS = 256
EXPERT_K_CHUNKS = 8
COMBINE_TOKENS = 64


def _tile(m, pref, mult=8):
    for t in range(min(pref, m), 0, -1):
        if m % t == 0 and t % mult == 0:
            return t
    return m


def _params(sem):
    return pltpu.CompilerParams(dimension_semantics=sem, vmem_limit_bytes=VMEM_LIMIT)


def _sigmoid(x):
    return 1.0 / (1.0 + jnp.exp(-x))


def _pack_halves(y):
    w = y.shape[1] // 2
    lo = lax.bitcast_convert_type(y[:, :w].astype(BF16).astype(F32), U32) >> 16
    hi = lax.bitcast_convert_type(y[:, w:].astype(BF16).astype(F32), U32) & jnp.uint32(0xFFFF0000)
    return lo | hi


def _unpack_halves(p):
    lo = lax.bitcast_convert_type(p << 16, F32)
    hi = lax.bitcast_convert_type(p & jnp.uint32(0xFFFF0000), F32)
    return lo, hi


def _adaln_kernel(s_ref, w_ref, b_ref, o_ref):
    d = w_ref.shape[1]
    tn = w_ref.shape[2]
    ch = _tile(d, 256)

    def body(r, acc):
        a0, a1 = acc
        r0 = pl.multiple_of(r * ch, ch)
        w = w_ref[0, pl.ds(r0, ch), :]
        s = s_ref[pl.ds(r0, ch), :]
        s = s * _sigmoid(s)
        p0 = (w * s[:, 0:1]).reshape(ch // 8, 8, tn).sum(axis=0)
        p1 = (w * s[:, 1:2]).reshape(ch // 8, 8, tn).sum(axis=0)
        return a0 + p0, a1 + p1

    z = jnp.zeros((8, tn), F32)
    a0, a1 = lax.fori_loop(0, d // ch, body, (z, z))
    o_ref[0, 0:1, :] = a0.sum(axis=0, keepdims=True) + b_ref[0]
    o_ref[0, 1:2, :] = a1.sum(axis=0, keepdims=True) + b_ref[0]


def _adaln(s_cols, ada_w, ada_b):
    depth, d, n6 = ada_w.shape
    tn = _tile(n6, 512, 128)
    return pl.pallas_call(
        _adaln_kernel,
        grid=(depth, n6 // tn),
        in_specs=[
            pl.BlockSpec((d, 2), lambda l, j: (0, 0)),
            pl.BlockSpec((1, d, tn), lambda l, j: (l, 0, j)),
            pl.BlockSpec((1, 1, tn), lambda l, j: (l, 0, j)),
        ],
        out_specs=pl.BlockSpec((1, 2, tn), lambda l, j: (l, 0, j)),
        out_shape=jax.ShapeDtypeStruct((depth, 2, n6), F32),
        compiler_params=_params(("arbitrary", "arbitrary")),
        name="adaln",
    )(s_cols, ada_w, ada_b.reshape(depth, 1, n6))


def _normmod_kernel(x_ref, g_ref, sc_ref, sh_ref, *o_refs):
    x = x_ref[...]
    ms = jnp.mean(x * x, axis=-1, keepdims=True)
    y = x * lax.rsqrt(ms + EPS) * g_ref[...]
    h = y * (1.0 + sc_ref[...]) + sh_ref[...]
    o_refs[0][...] = h.astype(BF16)
    if len(o_refs) > 1:
        o_refs[1][...] = _pack_halves(h)


def _normmod(x, g, sc, sh, packed=False):
    m, d = x.shape
    tm = _tile(m, 256)
    vec = pl.BlockSpec((1, d), lambda i: (0, 0))
    out_shape = [jax.ShapeDtypeStruct((m, d), BF16)]
    out_specs = [pl.BlockSpec((tm, d), lambda i: (i, 0))]
    if packed:
        out_shape.append(jax.ShapeDtypeStruct((m, d // 2), U32))
        out_specs.append(pl.BlockSpec((tm, d // 2), lambda i: (i, 0)))
    res = pl.pallas_call(
        _normmod_kernel,
        grid=(m // tm,),
        in_specs=[pl.BlockSpec((tm, d), lambda i: (i, 0)), vec, vec, vec],
        out_specs=out_specs,
        out_shape=out_shape,
        compiler_params=_params(("arbitrary",)),
        name="normmod",
    )(x, g.reshape(1, d), sc.reshape(1, d), sh.reshape(1, d))
    return res if packed else res[0]


def _normmod_pair_kernel(xa_ref, xb_ref, g_ref, sca_ref, sha_ref, scb_ref, shb_ref, o_ref, op_ref, *, na_blocks):
    i = pl.program_id(0)

    @pl.when(i < na_blocks)
    def _():
        _normmod_kernel(xa_ref, g_ref, sca_ref, sha_ref, o_ref, op_ref)

    @pl.when(i >= na_blocks)
    def _():
        _normmod_kernel(xb_ref, g_ref, scb_ref, shb_ref, o_ref, op_ref)


def _normmod_pair(xa, xb, g, sca, sha, scb, shb):
    ma, d = xa.shape
    mb = xb.shape[0]
    tm = _tile(math.gcd(ma, mb), 256)
    nab = ma // tm
    vec = pl.BlockSpec((1, d), lambda i: (0, 0))
    return pl.pallas_call(
        functools.partial(_normmod_pair_kernel, na_blocks=nab),
        grid=((ma + mb) // tm,),
        in_specs=[pl.BlockSpec((tm, d), lambda i: (jnp.minimum(i, nab - 1), 0)),
                  pl.BlockSpec((tm, d), lambda i: (jnp.maximum(i - nab, 0), 0)),
                  vec, vec, vec, vec, vec],
        out_specs=[pl.BlockSpec((tm, d), lambda i: (i, 0)), pl.BlockSpec((tm, d // 2), lambda i: (i, 0))],
        out_shape=[jax.ShapeDtypeStruct((ma + mb, d), BF16), jax.ShapeDtypeStruct((ma + mb, d // 2), U32)],
        compiler_params=_params(("arbitrary",)),
        name="normmod_pair",
    )(xa, xb, g.reshape(1, d), sca.reshape(1, d), sha.reshape(1, d), scb.reshape(1, d), shb.reshape(1, d))


def _mm_kernel(a_ref, b_ref, o_ref):
    o_ref[...] = jnp.dot(a_ref[...], b_ref[...], preferred_element_type=F32).astype(o_ref.dtype)


def _matmul(a, b, out_dtype, col_block_off=0, n_out=None, tm_pref=1024, tn_pref=1024):
    m, k = a.shape
    n_out = b.shape[1] if n_out is None else n_out
    tm = _tile(m, tm_pref)
    tn = _tile(n_out, tn_pref, 128)
    return pl.pallas_call(
        _mm_kernel,
        grid=(m // tm, n_out // tn),
        in_specs=[
            pl.BlockSpec((tm, k), lambda i, j: (i, 0)),
            pl.BlockSpec((k, tn), lambda i, j: (0, j + col_block_off)),
        ],
        out_specs=pl.BlockSpec((tm, tn), lambda i, j: (i, j)),
        out_shape=jax.ShapeDtypeStruct((m, n_out), out_dtype),
        compiler_params=_params(("arbitrary", "arbitrary")),
        name="matmul",
    )(a, b)


def _head_norm_rope(x, g, c, se, so):
    ms = jnp.mean(x * x, axis=-1, keepdims=True)
    y = x * lax.rsqrt(ms + EPS) * g
    return y * c + pltpu.roll(y, HEAD_DIM - 1, 1) * se + pltpu.roll(y, 1, 1) * so


def _qprep_kernel(p_ref, c_ref, se_ref, so_ref, g_ref, o_ref, *, nheads, scale):
    c, se, so, g = c_ref[...], se_ref[...], so_ref[...], g_ref[...]
    for h in range(nheads):
        sl = slice(h * HEAD_DIM, (h + 1) * HEAD_DIM)
        r = _head_norm_rope(p_ref[:, sl].astype(F32), g, c, se, so) * scale
        o_ref[sl, :] = r.T.astype(o_ref.dtype)


def _qprep(p, tabs, g):
    m = p.shape[0]
    tq = _tile(m, 512, 128)
    tab = pl.BlockSpec((tq, HEAD_DIM), lambda i: (i, 0))
    return pl.pallas_call(
        functools.partial(_qprep_kernel, nheads=GQA_HEADS, scale=HEAD_DIM ** -0.5 * LOG2E),
        grid=(m // tq,),
        in_specs=[pl.BlockSpec((tq, GQA_WIDTH), lambda i: (i, QA_OFF // GQA_WIDTH)), tab, tab, tab,
                  pl.BlockSpec((1, HEAD_DIM), lambda i: (0, 0))],
        out_specs=pl.BlockSpec((GQA_WIDTH, tq), lambda i: (0, i)),
        out_shape=jax.ShapeDtypeStruct((GQA_WIDTH, m), BF16),
        compiler_params=_params(("arbitrary",)),
        name="qprep",
    )(p, *tabs, g.reshape(1, HEAD_DIM))


VT_ROWS = HEAD_DIM + 16


def _kvprep_kernel(pc_ref, pl_ref, c_ref, se_ref, so_ref, g_ref, k_ref, vt_ref, *, n_ctx_blocks):
    i = pl.program_id(0)
    c, se, so, g = c_ref[...], se_ref[...], so_ref[...], g_ref[...]
    t = k_ref.shape[0]
    ones_row = jnp.where(lax.broadcasted_iota(I32, (VT_ROWS - HEAD_DIM, t), 0) == 0, 1.0, 0.0).astype(BF16)

    def run(p_ref):
        for h in range(GQA_KV_HEADS):
            sl = slice(h * HEAD_DIM, (h + 1) * HEAD_DIM)
            k_ref[:, sl] = _head_norm_rope(p_ref[:, sl].astype(F32), g, c, se, so).astype(k_ref.dtype)
            v = p_ref[:, GQA_KV_WIDTH + h * HEAD_DIM:GQA_KV_WIDTH + (h + 1) * HEAD_DIM].astype(F32)
            vt_ref[h * VT_ROWS:h * VT_ROWS + HEAD_DIM, :] = v.T.astype(vt_ref.dtype)
            vt_ref[h * VT_ROWS + HEAD_DIM:(h + 1) * VT_ROWS, :] = ones_row

    @pl.when(i < n_ctx_blocks)
    def _():
        run(pc_ref)

    @pl.when(i >= n_ctx_blocks)
    def _():
        run(pl_ref)


def _kvprep(p_ctx, p_lat, tabs, g):
    l, n = p_ctx.shape[0], p_lat.shape[0]
    t = _tile(math.gcd(l, n), 256, 128)
    ncb = l // t
    w = 2 * GQA_KV_WIDTH
    cb = KA_OFF // w
    tab = pl.BlockSpec((t, HEAD_DIM), lambda i: (i, 0))
    return pl.pallas_call(
        functools.partial(_kvprep_kernel, n_ctx_blocks=ncb),
        grid=((l + n) // t,),
        in_specs=[pl.BlockSpec((t, w), lambda i: (jnp.minimum(i, ncb - 1), cb)),
                  pl.BlockSpec((t, w), lambda i: (jnp.maximum(i - ncb, 0), cb)),
                  tab, tab, tab, pl.BlockSpec((1, HEAD_DIM), lambda i: (0, 0))],
        out_specs=[pl.BlockSpec((t, GQA_KV_WIDTH), lambda i: (i, 0)),
                   pl.BlockSpec((GQA_KV_HEADS * VT_ROWS, t), lambda i: (0, i))],
        out_shape=[jax.ShapeDtypeStruct((l + n, GQA_KV_WIDTH), BF16),
                   jax.ShapeDtypeStruct((GQA_KV_HEADS * VT_ROWS, l + n), BF16)],
        compiler_params=_params(("arbitrary",)),
        name="kvprep",
    )(p_ctx, p_lat, *tabs, g.reshape(1, HEAD_DIM))


def _gqa_flash_kernel(qT_ref, k_ref, vT_ref, o_ref, m_ref, acc_ref, *, group, cb):
    j = pl.program_id(2)
    tq = qT_ref.shape[1]

    @pl.when(j == 0)
    def _():
        m_ref[...] = jnp.full(m_ref.shape, -jnp.inf, F32)
        acc_ref[...] = jnp.zeros(acc_ref.shape, F32)

    k = k_ref[...]
    vT = vT_ref[...]
    nsub = tq // cb
    nblk = group * nsub
    st = {}

    def scores(c):
        h, sub = divmod(c, nsub)
        cols = slice(c * cb, (c + 1) * cb)
        qT = qT_ref[h * HEAD_DIM:(h + 1) * HEAD_DIM, sub * cb:(sub + 1) * cb]
        sT = jnp.dot(k, qT, preferred_element_type=F32)
        m_prev = m_ref[:, cols]
        m_new = jnp.maximum(m_prev, jnp.max(sT, axis=0, keepdims=True))
        m_ref[:, cols] = m_new
        st[c] = (sT, m_new, jnp.exp2(m_prev - m_new))

    def probs(c):
        sT, m_new, alpha = st[c]
        st[c] = (jnp.exp2(sT - m_new).astype(BF16), alpha)

    def accumulate(c):
        cols = slice(c * cb, (c + 1) * cb)
        pT, alpha = st.pop(c)
        acc_ref[:, cols] = alpha * acc_ref[:, cols] + jnp.dot(vT, pT, preferred_element_type=F32)

    scores(0)
    for c in range(nblk):
        if c + 1 < nblk:
            scores(c + 1)
        if c >= 1:
            accumulate(c - 1)
        probs(c)
    accumulate(nblk - 1)

    @pl.when(j == pl.num_programs(2) - 1)
    def _():
        for h in range(group):
            cols = slice(h * tq, (h + 1) * tq)
            o = acc_ref[:HEAD_DIM, cols] * (1.0 / acc_ref[HEAD_DIM:HEAD_DIM + 1, cols])
            o_ref[:, h * HEAD_DIM:(h + 1) * HEAD_DIM] = o.T.astype(o_ref.dtype)


def _gqa_flash(qT, k, vT, tq_pref=1024, tk_pref=3328, cb=512):
    nq, tk_total = qT.shape[1], k.shape[0]
    group = GQA_HEADS // GQA_KV_HEADS
    tq = _tile(nq, tq_pref, 128)
    tk = _tile(tk_total, tk_pref, 128)
    gw = group * HEAD_DIM
    return pl.pallas_call(
        functools.partial(_gqa_flash_kernel, group=group, cb=min(cb, tq)),
        grid=(GQA_KV_HEADS, nq // tq, tk_total // tk),
        in_specs=[
            pl.BlockSpec((gw, tq), lambda g, i, j: (g, i)),
            pl.BlockSpec((tk, HEAD_DIM), lambda g, i, j: (j, g)),
            pl.BlockSpec((VT_ROWS, tk), lambda g, i, j: (g, j)),
        ],
        out_specs=pl.BlockSpec((tq, gw), lambda g, i, j: (i, g)),
        out_shape=jax.ShapeDtypeStruct((nq, GQA_WIDTH), BF16),
        scratch_shapes=[
            pltpu.VMEM((1, group * tq), F32),
            pltpu.VMEM((VT_ROWS, group * tq), F32),
        ],
        compiler_params=_params(("arbitrary", "arbitrary", "arbitrary")),
        name="gqa_flash",
    )(qT, k, vT)


def _flash_kernel(q_ref, k_ref, v_ref, o_ref, qs_ref, m_ref, l_ref, acc_ref, *, group, scale):
    j = pl.program_id(2)
    tq = q_ref.shape[0]

    @pl.when(j == 0)
    def _():
        for h in range(group):
            sl = slice(h * HEAD_DIM, (h + 1) * HEAD_DIM)
            qs_ref[h * tq:(h + 1) * tq, :] = (q_ref[:, sl].astype(F32) * scale).astype(BF16)
        m_ref[...] = jnp.full(m_ref.shape, -jnp.inf, F32)
        l_ref[...] = jnp.zeros(l_ref.shape, F32)
        acc_ref[...] = jnp.zeros(acc_ref.shape, F32)

    s = lax.dot_general(qs_ref[...], k_ref[...], (((1,), (1,)), ((), ())), preferred_element_type=F32)
    m_prev = m_ref[...]
    m_new = jnp.maximum(m_prev, jnp.max(s, axis=-1, keepdims=True))
    alpha = jnp.exp2(m_prev - m_new)
    p = jnp.exp2(s - m_new)
    l_ref[...] = alpha * l_ref[...] + jnp.sum(p, axis=-1, keepdims=True)
    acc_ref[...] = alpha * acc_ref[...] + jnp.dot(p.astype(BF16), v_ref[...], preferred_element_type=F32)
    m_ref[...] = m_new

    @pl.when(j == pl.num_programs(2) - 1)
    def _():
        inv = 1.0 / l_ref[...]
        for h in range(group):
            sl = slice(h * HEAD_DIM, (h + 1) * HEAD_DIM)
            o_ref[:, sl] = (acc_ref[h * tq:(h + 1) * tq, :] * inv[h * tq:(h + 1) * tq]).astype(o_ref.dtype)


def _flash(q, q_off, k, k_off, v, v_off, n_kv_heads, group, tq_pref=512, tk_pref=1280):
    nq, tk_total = q.shape[0], k.shape[0]
    tq = _tile(nq, tq_pref)
    tk = _tile(tk_total, tk_pref, 128)
    gw = group * HEAD_DIM
    scale = HEAD_DIM ** -0.5 * LOG2E
    return pl.pallas_call(
        functools.partial(_flash_kernel, group=group, scale=scale),
        grid=(n_kv_heads, nq // tq, tk_total // tk),
        in_specs=[
            pl.BlockSpec((tq, gw), lambda g, i, j: (i, q_off // gw + g)),
            pl.BlockSpec((tk, HEAD_DIM), lambda g, i, j: (j, k_off // HEAD_DIM + g)),
            pl.BlockSpec((tk, HEAD_DIM), lambda g, i, j: (j, v_off // HEAD_DIM + g)),
        ],
        out_specs=pl.BlockSpec((tq, gw), lambda g, i, j: (i, g)),
        out_shape=jax.ShapeDtypeStruct((nq, n_kv_heads * gw), BF16),
        scratch_shapes=[
            pltpu.VMEM((group * tq, HEAD_DIM), BF16),
            pltpu.VMEM((group * tq, 1), F32),
            pltpu.VMEM((group * tq, 1), F32),
            pltpu.VMEM((group * tq, HEAD_DIM), F32),
        ],
        compiler_params=_params(("arbitrary", "arbitrary", "arbitrary")),
        name="flash",
    )(q, k, v)


NA_QROWS = 8
NA_BAND = 16


def _na_kernel(q_ref, k0, k1, k2, k3, v0, v1, v2, v3, kc_ref, vc_ref, tb_ref, o_ref, kb_ref, vb_ref,
               *, rows):
    b = pl.program_id(0)
    quarter = NA_BAND * GRID_W // 4
    for d, (kr, vr) in enumerate(((k0, v0), (k1, v1), (k2, v2), (k3, v3))):
        kb_ref[d * quarter:(d + 1) * quarter, :] = kr[...]
        vb_ref[d * quarter:(d + 1) * quarter, :] = vr[...]
    band_start = jnp.clip(NA_QROWS * b - NA_WIN_ROWS // 2, 0, rows - NA_BAND)
    scale = HEAD_DIM ** -0.5
    win = NA_WIN_ROWS * GRID_W

    def body(a, carry):
        r = NA_QROWS * b + a
        r_start = jnp.clip(r - NA_WIN_ROWS // 2, 0, rows - NA_WIN_ROWS)
        dd = r - r_start
        koff = pl.multiple_of((r_start - band_start) * GRID_W, GRID_W)
        qoff = pl.multiple_of(a * GRID_W, GRID_W)
        q_a = q_ref[pl.ds(qoff, GRID_W), :]
        kw = kb_ref[pl.ds(koff, win), :]
        vw = vb_ref[pl.ds(koff, win), :]
        for h in range(NA_HEADS):
            sl = slice(h * HEAD_DIM, (h + 1) * HEAD_DIM)
            qh = q_a[:, sl]
            dn = (((1,), (1,)), ((), ()))
            s_nb = lax.dot_general(qh, kw[:, sl], dn, preferred_element_type=F32) * scale + tb_ref[h, dd]
            s_c = lax.dot_general(qh, kc_ref[:, sl], dn, preferred_element_type=F32) * scale
            m = jnp.maximum(jnp.max(s_nb, axis=-1, keepdims=True), jnp.max(s_c, axis=-1, keepdims=True))
            p_nb = jnp.exp(s_nb - m)
            p_c = jnp.exp(s_c - m)
            l = jnp.sum(p_nb, axis=-1, keepdims=True) + jnp.sum(p_c, axis=-1, keepdims=True)
            o = (jnp.dot(p_c.astype(BF16), vc_ref[:, sl], preferred_element_type=F32)
                 + jnp.dot(p_nb.astype(BF16), vw[:, sl], preferred_element_type=F32))
            o_ref[pl.ds(qoff, GRID_W), sl] = (o / l).astype(o_ref.dtype)
        return carry

    lax.fori_loop(0, NA_QROWS, body, 0)


def _na_bias_table(rel_bias):
    h = rel_bias.shape[0]
    wr, wc = NA_WIN_ROWS, NA_WIN_COLS
    nro, nco = 2 * wr - 1, 2 * wc - 1
    p = 2 * GRID_W
    w = jnp.full((h, nro, p), MASK_VALUE, F32).at[:, :, :nco].set(rel_bias.astype(F32))
    sk = jnp.broadcast_to(w[:, :, None, :], (h, nro, GRID_W, p)).reshape(h, nro, GRID_W * p)
    sk = sk[:, :, :GRID_W * (p - 1)].reshape(h, nro, GRID_W, p - 1)[:, :, :, wc - 1:wc - 1 + GRID_W]
    qc = np.arange(GRID_W)
    kc = np.arange(GRID_W)
    col_start = np.clip(qc - wc // 2, 0, GRID_W - wc)
    valid = (kc[None, :] >= col_start[:, None]) & (kc[None, :] < col_start[:, None] + wc)
    sk = jnp.where(jnp.asarray(valid)[None, None], sk, MASK_VALUE)
    tb = jnp.stack([sk[:, wr - 1 - dd:2 * wr - 1 - dd] for dd in range(wr)], axis=1)
    return tb.transpose(0, 1, 3, 2, 4).reshape(h, wr, GRID_W, wr * GRID_W)


def _na(p_lat, p_ctx, rel_bias):
    n, l = p_lat.shape[0], p_ctx.shape[0]
    rows = n // GRID_W
    assert rows % NA_QROWS == 0 and rows >= NA_BAND
    tq = NA_QROWS * GRID_W
    quarter = NA_BAND * GRID_W // 4
    nquart = n // quarter

    def band_block(d, col):
        return pl.BlockSpec((quarter, NA_WIDTH),
                            lambda b: (jnp.clip(2 * b - 1, 0, nquart - 4) + d, col))

    kcol, vcol = KN_OFF // NA_WIDTH, VN_OFF // NA_WIDTH
    tb = _na_bias_table(rel_bias)
    return pl.pallas_call(
        functools.partial(_na_kernel, rows=rows),
        grid=(rows // NA_QROWS,),
        in_specs=[pl.BlockSpec((tq, NA_WIDTH), lambda b: (b, QN_OFF // NA_WIDTH))]
        + [band_block(d, kcol) for d in range(4)]
        + [band_block(d, vcol) for d in range(4)]
        + [pl.BlockSpec((l, NA_WIDTH), lambda b: (0, kcol)),
           pl.BlockSpec((l, NA_WIDTH), lambda b: (0, vcol)),
           pl.BlockSpec(tb.shape, lambda b: (0, 0, 0, 0))],
        out_specs=pl.BlockSpec((tq, NA_WIDTH), lambda b: (b, 0)),
        out_shape=jax.ShapeDtypeStruct((n, NA_WIDTH), BF16),
        scratch_shapes=[pltpu.VMEM((NA_BAND * GRID_W, NA_WIDTH), BF16),
                        pltpu.VMEM((NA_BAND * GRID_W, NA_WIDTH), BF16)],
        compiler_params=_params(("arbitrary",)),
        name="natten",
    )(p_lat, *([p_lat] * 8), p_ctx, p_ctx, tb)


def _dft_cs(n):
    k = np.arange(n, dtype=np.float64)
    ang = 2.0 * np.pi * np.outer(k, k) / n
    return np.cos(ang), np.sin(ang)


def _f1_kernel(x_ref, da_ref, db_ref, twr_ref, twi_ref, u_ref, *, tb, n_hi):
    w = FOURIER_WIDTH
    for j in range(tb):
        x = x_ref[:, j, :].astype(BF16)
        zs = [jnp.dot(x[:, g * HEAD_DIM:(g + 1) * HEAD_DIM], da_ref[...], preferred_element_type=F32)
              for g in range(FOURIER_GROUPS)]
        zr = jnp.concatenate([z[:, :HEAD_DIM] for z in zs], axis=1)
        zi = jnp.concatenate([z[:, HEAD_DIM:] for z in zs], axis=1)
        zst = jnp.concatenate([zr, zi], axis=0).astype(BF16)
        u = jnp.dot(db_ref[...], zst, preferred_element_type=F32)
        ur, ui = u[:n_hi], u[n_hi:]
        tr, ti = twr_ref[j], twi_ref[j]
        u_ref[:, j, :w] = ur * tr - ui * ti
        u_ref[:, j, w:] = ur * ti + ui * tr


def _f2_kernel(u_ref, dc_ref, y_ref, *, ta):
    w = FOURIER_WIDTH
    for j in range(ta):
        u = u_ref[j]
        ust = jnp.concatenate([u[:, :w], u[:, w:]], axis=0).astype(BF16)
        y_ref[:, j, :] = jnp.dot(dc_ref[...], ust, preferred_element_type=F32)


def _fourier(xf):
    n, w = xf.shape
    n_lo = 128
    n_hi = n // n_lo
    assert n_hi * n_lo == n and n_hi % 8 == 0
    cc, sc = _dft_cs(HEAD_DIM)
    da = jnp.asarray(np.concatenate([cc, -sc], axis=1), BF16)
    cn, sn = _dft_cs(n_hi)
    db = jnp.asarray(np.block([[cn, sn], [-sn, cn]]), BF16)
    ang = 2.0 * np.pi * np.outer(np.arange(n_lo), np.arange(n_hi)) / n
    twr = jnp.asarray(np.cos(ang)[:, :, None], F32)
    twi = jnp.asarray(-np.sin(ang)[:, :, None], F32)
    cl, sl = _dft_cs(n_lo)
    dc = jnp.asarray(np.concatenate([cl, sl], axis=1), BF16)
    tb = 8
    u = pl.pallas_call(
        functools.partial(_f1_kernel, tb=tb, n_hi=n_hi),
        grid=(n_lo // tb,),
        in_specs=[pl.BlockSpec((n_hi, tb, w), lambda j: (0, j, 0)),
                  pl.BlockSpec(da.shape, lambda j: (0, 0)),
                  pl.BlockSpec(db.shape, lambda j: (0, 0)),
                  pl.BlockSpec((tb, n_hi, 1), lambda j: (j, 0, 0)),
                  pl.BlockSpec((tb, n_hi, 1), lambda j: (j, 0, 0))],
        out_specs=pl.BlockSpec((n_hi, tb, 2 * w), lambda j: (0, j, 0)),
        out_shape=jax.ShapeDtypeStruct((n_hi, n_lo, 2 * w), F32),
        compiler_params=_params(("arbitrary",)),
        name="fourier_stage1",
    )(xf.reshape(n_hi, n_lo, w), da, db, twr, twi)
    ta = 8
    y = pl.pallas_call(
        functools.partial(_f2_kernel, ta=ta),
        grid=(n_hi // ta,),
        in_specs=[pl.BlockSpec((ta, n_lo, 2 * w), lambda i: (i, 0, 0)),
                  pl.BlockSpec(dc.shape, lambda i: (0, 0))],
        out_specs=pl.BlockSpec((n_lo, ta, w), lambda i: (0, i, 0)),
        out_shape=jax.ShapeDtypeStruct((n_lo, n_hi, w), F32),
        compiler_params=_params(("arbitrary",)),
        name="fourier_stage2",
    )(u, dc)
    return y.reshape(n, w)


def _fourier_small_kernel(x_ref, dch_ref, dpos_ref, y_ref):
    x = x_ref[...].astype(BF16)
    for g in range(FOURIER_GROUPS):
        sl = slice(g * HEAD_DIM, (g + 1) * HEAD_DIM)
        a = jnp.dot(x[:, sl], dch_ref[...], preferred_element_type=F32)
        ast = jnp.concatenate([a[:, :HEAD_DIM], a[:, HEAD_DIM:]], axis=0).astype(BF16)
        y_ref[:, sl] = jnp.dot(dpos_ref[...], ast, preferred_element_type=F32)


def _fourier_small(xf):
    t, w = xf.shape
    cc, sc = _dft_cs(HEAD_DIM)
    dch = jnp.asarray(np.concatenate([cc, sc], axis=1), BF16)
    cp, sp = _dft_cs(t)
    dpos = jnp.asarray(np.concatenate([cp, -sp], axis=1), BF16)
    return pl.pallas_call(
        _fourier_small_kernel,
        grid=(1,),
        in_specs=[pl.BlockSpec((t, w), lambda i: (0, 0)),
                  pl.BlockSpec(dch.shape, lambda i: (0, 0)),
                  pl.BlockSpec(dpos.shape, lambda i: (0, 0))],
        out_specs=pl.BlockSpec((t, w), lambda i: (0, 0)),
        out_shape=jax.ShapeDtypeStruct((t, w), F32),
        compiler_params=_params(("arbitrary",)),
        name="fourier_small",
    )(xf, dch, dpos)


def _outproj_kernel(og_ref, on_ref, of_ref, g_ref, w_ref, x_ref, gate_ref, o_ref, m_ref):
    @pl.when(pl.program_id(1) == 0)
    def _():
        def nrm(v, g):
            vf = v.astype(F32)
            return (vf * lax.rsqrt(jnp.mean(vf * vf, axis=-1, keepdims=True) + EPS) * g).astype(BF16)

        a, b = GQA_WIDTH, GQA_WIDTH + NA_WIDTH
        m_ref[:, :a] = nrm(og_ref[...], g_ref[:, :a])
        m_ref[:, a:b] = nrm(on_ref[...], g_ref[:, a:b])
        m_ref[:, b:] = nrm(of_ref[...], g_ref[:, b:])

    acc = jnp.dot(m_ref[...], w_ref[...], preferred_element_type=F32)
    o_ref[...] = x_ref[...] + gate_ref[...] * acc


def _outproj(o_gqa, o_na, o_four, g_out, w_out, x, gate):
    m, d = x.shape
    tm = _tile(m, 512)
    tn = _tile(d, 1024, 128)
    return pl.pallas_call(
        _outproj_kernel,
        grid=(m // tm, d // tn),
        in_specs=[pl.BlockSpec((tm, GQA_WIDTH), lambda i, j: (i, 0)),
                  pl.BlockSpec((tm, NA_WIDTH), lambda i, j: (i, 0)),
                  pl.BlockSpec((tm, FOURIER_WIDTH), lambda i, j: (i, 0)),
                  pl.BlockSpec((1, MIX_WIDTH), lambda i, j: (0, 0)),
                  pl.BlockSpec((MIX_WIDTH, tn), lambda i, j: (0, j)),
                  pl.BlockSpec((tm, tn), lambda i, j: (i, j)),
                  pl.BlockSpec((1, tn), lambda i, j: (0, j))],
        out_specs=pl.BlockSpec((tm, tn), lambda i, j: (i, j)),
        out_shape=jax.ShapeDtypeStruct((m, d), F32),
        scratch_shapes=[pltpu.VMEM((tm, MIX_WIDTH), BF16)],
        compiler_params=_params(("arbitrary", "arbitrary")),
        name="outproj",
    )(o_gqa, o_na, o_four, g_out.reshape(1, MIX_WIDTH), w_out, x, gate.reshape(1, d))


def _router_kernel(h_ref, rw_ref, rb_ref, idx_ref, wt_ref, rank_ref, cnt_ref, carry_ref):
    step = pl.program_id(0)

    @pl.when(step == 0)
    def _():
        carry_ref[...] = jnp.zeros(carry_ref.shape, F32)

    tm = h_ref.shape[0]
    e = N_EXPERTS
    gsz = e // N_EXPERT_GROUPS
    neg = -jnp.inf
    logits = jnp.dot(h_ref[...], rw_ref[...], preferred_element_type=F32)
    scores = _sigmoid(logits)
    biased = scores + rb_ref[...]
    lane = lax.broadcasted_iota(I32, (tm, e), 1).astype(F32)
    lane_grp = lax.broadcasted_iota(I32, (tm, e), 1) // gsz

    def first_argmax(v):
        m = jnp.max(v, axis=-1, keepdims=True)
        return m, jnp.min(jnp.where(v == m, lane, float(e)), axis=-1, keepdims=True)

    gs = []
    for g in range(N_EXPERT_GROUPS):
        vg = jnp.where(lane_grp == g, biased, neg)
        m1, i1 = first_argmax(vg)
        m2 = jnp.max(jnp.where(lane == i1, neg, vg), axis=-1, keepdims=True)
        gs.append(m1 + m2)
    masked = jnp.full((tm, e), neg, F32)
    for g in range(N_EXPERT_GROUPS):
        ahead = jnp.zeros((tm, 1), F32)
        for g2 in range(N_EXPERT_GROUPS):
            if g2 == g:
                continue
            better = (gs[g2] > gs[g]) | ((gs[g2] == gs[g]) & (g2 < g))
            ahead = ahead + jnp.where(better, 1.0, 0.0)
        masked = jnp.where((lane_grp == g) & (ahead < TOPK_GROUPS), biased, masked)
    sel = jnp.zeros((tm, e), F32)
    idxs, ws = [], []
    cur = masked
    for _ in range(TOP_K):
        _, ik = first_argmax(cur)
        hit = lane == ik
        ws.append(jnp.sum(jnp.where(hit, scores, 0.0), axis=-1, keepdims=True))
        cur = jnp.where(hit, neg, cur)
        sel = sel + jnp.where(hit, 1.0, 0.0)
        idxs.append(ik)
    wsum = ws[0]
    for wk in ws[1:]:
        wsum = wsum + wk
    rr = lax.broadcasted_iota(I32, (tm, tm), 0)
    cc = lax.broadcasted_iota(I32, (tm, tm), 1)
    tri = jnp.where(cc < rr, 1.0, 0.0).astype(BF16)
    pref = jnp.dot(tri, sel.astype(BF16), preferred_element_type=F32) + carry_ref[...]
    ranks = [jnp.sum(jnp.where(lane == ik, pref, 0.0), axis=-1, keepdims=True) for ik in idxs]
    carry_ref[...] = carry_ref[...] + jnp.sum(sel, axis=0, keepdims=True)

    lane_o = lax.broadcasted_iota(I32, (tm, 128), 1)
    o_idx = jnp.zeros((tm, 128), F32)
    o_w = jnp.zeros((tm, 128), F32)
    o_rank = jnp.zeros((tm, 128), F32)
    for k in range(TOP_K):
        o_idx = jnp.where(lane_o == k, idxs[k], o_idx)
        o_w = jnp.where(lane_o == k, ws[k] / wsum * ROUTED_SCALE, o_w)
        o_rank = jnp.where(lane_o == k, ranks[k], o_rank)
    idx_ref[...] = o_idx.astype(I32)
    wt_ref[...] = o_w
    rank_ref[...] = o_rank.astype(I32)
    cnt_ref[...] = jnp.broadcast_to(carry_ref[...], cnt_ref.shape)


def _router(h, rw, rb):
    t, d = h.shape
    tm = _tile(t, 256)
    blk = pl.BlockSpec((tm, 128), lambda i: (i, 0))
    idx, wt, rank, cnt = pl.pallas_call(
        _router_kernel,
        grid=(t // tm,),
        in_specs=[pl.BlockSpec((tm, d), lambda i: (i, 0)),
                  pl.BlockSpec((d, N_EXPERTS), lambda i: (0, 0)),
                  pl.BlockSpec((1, N_EXPERTS), lambda i: (0, 0))],
        out_specs=[blk, blk, blk, pl.BlockSpec((8, N_EXPERTS), lambda i: (0, 0))],
        out_shape=[jax.ShapeDtypeStruct((t, 128), I32), jax.ShapeDtypeStruct((t, 128), F32),
                   jax.ShapeDtypeStruct((t, 128), I32), jax.ShapeDtypeStruct((8, N_EXPERTS), F32)],
        scratch_shapes=[pltpu.VMEM((1, N_EXPERTS), F32)],
        compiler_params=_params(("arbitrary",)),
        name="router",
    )(h, rw, rb.reshape(1, N_EXPERTS))
    return idx[:, :TOP_K], wt[:, :TOP_K], rank[:, :TOP_K], cnt[0].astype(I32)


def _expert_kernel(be_ref, nu_ref, rt_hbm, hp_hbm, wgu_ref, wd_ref, o_ref, idx_smem, xbuf, idx_sem, x_sem):
    b = pl.program_id(0)
    n_used = nu_ref[0]
    n_blocks = pl.num_programs(0)
    bm = xbuf.shape[1]
    dh = xbuf.shape[2]
    ff = wd_ref.shape[1]
    slot = b % 2
    nslot = 1 - slot
    n_chunks = max(1, min(EXPERT_K_CHUNKS, dh // 128))
    kc = dh // n_chunks
    per = bm // n_chunks

    def idx_copy(s, sl):
        return pltpu.make_async_copy(rt_hbm.at[s], idx_smem.at[sl], idx_sem.at[sl])

    def row_copy(sl, r, tok):
        return pltpu.make_async_copy(hp_hbm.at[pl.ds(tok, 1), :], xbuf.at[sl, pl.ds(r, 1), :], x_sem.at[sl])

    def slot_wait(sl):
        pltpu.make_async_copy(xbuf.at[sl], xbuf.at[sl], x_sem.at[sl]).wait()

    @pl.when(b == 0)
    def _():
        idx_copy(0, 0).start()
        idx_copy(0, 0).wait()

        def body(r, carry):
            row_copy(0, r, idx_smem[0, r]).start()
            return carry

        lax.fori_loop(0, bm, body, 0)
        idx_copy(1, 1).start()

    @pl.when(b < n_used)
    def _():
        idx_copy(b + 1, nslot).wait()
        slot_wait(slot)
        h = None
        for c in range(n_chunks):
            lo, hi = _unpack_halves(xbuf[slot, :, c * kc:(c + 1) * kc])
            part = (jnp.dot(lo.astype(BF16), wgu_ref[0, c * kc:(c + 1) * kc, :], preferred_element_type=F32)
                    + jnp.dot(hi.astype(BF16), wgu_ref[0, dh + c * kc:dh + (c + 1) * kc, :],
                              preferred_element_type=F32))
            h = part if h is None else h + part
            for r in range(c * per, (c + 1) * per):
                row_copy(nslot, r, idx_smem[nslot, r]).start()
        idx_copy(jnp.minimum(b + 2, n_blocks - 1), slot).start()
        gate, up = h[:, :ff], h[:, ff:]
        act = (gate * _sigmoid(gate) * up).astype(BF16)
        y = jnp.dot(act, wd_ref[0], preferred_element_type=F32)
        o_ref[...] = _pack_halves(y)

    @pl.when(b >= n_used)
    def _():
        o_ref[...] = jnp.zeros(o_ref.shape, o_ref.dtype)

    @pl.when(b == n_used)
    def _():
        idx_copy(0, nslot).wait()
        slot_wait(slot)


def _experts(hp, row_tok, block_e, n_used, wgu, wd):
    n_blocks, bm = row_tok.shape
    dh = hp.shape[1]
    d = 2 * dh
    ff = wd.shape[1]
    grid_spec = pltpu.PrefetchScalarGridSpec(
        num_scalar_prefetch=2,
        grid=(n_blocks,),
        in_specs=[pl.BlockSpec(memory_space=pl.ANY),
                  pl.BlockSpec(memory_space=pl.ANY),
                  pl.BlockSpec((1, d, 2 * ff), lambda b, be, nu: (be[b], 0, 0)),
                  pl.BlockSpec((1, ff, d), lambda b, be, nu: (be[b], 0, 0))],
        out_specs=pl.BlockSpec((bm, dh), lambda b, be, nu: (b, 0)),
        scratch_shapes=[pltpu.SMEM((2, bm), I32),
                        pltpu.VMEM((2, bm, dh), U32),
                        pltpu.SemaphoreType.DMA((2,)),
                        pltpu.SemaphoreType.DMA((2,))],
    )
    return pl.pallas_call(
        _expert_kernel,
        grid_spec=grid_spec,
        out_shape=jax.ShapeDtypeStruct((n_blocks * bm, dh), U32),
        compiler_params=_params(("arbitrary",)),
        name="experts",
    )(block_e, n_used, row_tok, hp, wgu, wd)


def _shared_kernel(h_ref, wgu_ref, wd_ref, o_ref):
    ff = wd_ref.shape[0]
    h = jnp.dot(h_ref[...], wgu_ref[...], preferred_element_type=F32)
    gate, up = h[:, :ff], h[:, ff:]
    act = (gate * _sigmoid(gate) * up).astype(BF16)
    o_ref[...] = jnp.dot(act, wd_ref[...], preferred_element_type=F32).astype(o_ref.dtype)


def _shared(h, wgu, wd):
    t, d = h.shape
    ff = wd.shape[0]
    tm = _tile(t, 512)
    return pl.pallas_call(
        _shared_kernel,
        grid=(t // tm,),
        in_specs=[pl.BlockSpec((tm, d), lambda i: (i, 0)),
                  pl.BlockSpec((d, 2 * ff), lambda i: (0, 0)),
                  pl.BlockSpec((ff, d), lambda i: (0, 0))],
        out_specs=pl.BlockSpec((tm, d), lambda i: (i, 0)),
        out_shape=jax.ShapeDtypeStruct((t, d), BF16),
        compiler_params=_params(("arbitrary",)),
        name="shared_expert",
    )(h, wgu, wd)


def _combine_kernel(dest_hbm, ys_hbm, w_ref, ysh_ref, x_ref, gate_ref, *rest, final):
    if final:
        fg_ref, o_ref, idx_smem, gbuf, idx_sem, g_sem = rest
    else:
        o_ref, idx_smem, gbuf, idx_sem, g_sem = rest
    s = pl.program_id(0)
    last = pl.num_programs(0) - 1
    tt = gbuf.shape[2]
    dh = gbuf.shape[3]
    slot = s % 2
    nslot = 1 - slot
    per = tt // TOP_K

    def idx_copy(st, sl):
        return pltpu.make_async_copy(dest_hbm.at[st], idx_smem.at[sl], idx_sem.at[sl])

    def row_copy(sl, k, t, row):
        return pltpu.make_async_copy(ys_hbm.at[pl.ds(row, 1), :], gbuf.at[sl, k, pl.ds(t, 1), :], g_sem.at[sl])

    def slot_wait(sl):
        pltpu.make_async_copy(gbuf.at[sl], gbuf.at[sl], g_sem.at[sl]).wait()

    @pl.when(s == 0)
    def _():
        idx_copy(0, 0).start()
        idx_copy(0, 0).wait()

        def body(t, carry):
            for k in range(TOP_K):
                row_copy(0, k, t, idx_smem[0, t * TOP_K + k]).start()
            return carry

        lax.fori_loop(0, tt, body, 0)
        idx_copy(jnp.minimum(1, last), 1).start()

    idx_copy(0, nslot).wait()
    slot_wait(slot)
    w = w_ref[...]
    acc_lo = jnp.zeros((tt, dh), F32)
    acc_hi = jnp.zeros((tt, dh), F32)
    for k in range(TOP_K):
        lo, hi = _unpack_halves(gbuf[slot, k])
        wk = w[:, k:k + 1]
        acc_lo = acc_lo + wk * lo
        acc_hi = acc_hi + wk * hi
        for t in range(k * per, (k + 1) * per):
            for kk in range(TOP_K):
                row_copy(nslot, kk, t, idx_smem[nslot, t * TOP_K + kk]).start()
    idx_copy(jnp.minimum(s + 2, last), slot).start()
    y = jnp.concatenate([acc_lo, acc_hi], axis=1) + ysh_ref[...].astype(F32)
    xn = x_ref[...] + gate_ref[...] * y
    if final:
        xn = xn * lax.rsqrt(jnp.mean(xn * xn, axis=-1, keepdims=True) + EPS) * fg_ref[...]
    o_ref[...] = xn

    @pl.when(s == last)
    def _():
        idx_copy(0, slot).wait()
        slot_wait(nslot)


def _combine(dest, w, ys, ysh, x, gate, final_g=None, ysh_row_off=0):
    t, d = x.shape
    tt = _tile(t, COMBINE_TOKENS)
    dh = d // 2
    final = final_g is not None
    off_blocks, rem = divmod(ysh_row_off, tt)
    assert rem == 0
    in_specs = [pl.BlockSpec(memory_space=pl.ANY),
                pl.BlockSpec(memory_space=pl.ANY),
                pl.BlockSpec((tt, TOP_K), lambda i: (i, 0)),
                pl.BlockSpec((tt, d), lambda i: (i + off_blocks, 0)),
                pl.BlockSpec((tt, d), lambda i: (i, 0)),
                pl.BlockSpec((1, d), lambda i: (0, 0))]
    args = [dest.reshape(t // tt, tt * TOP_K), ys, w, ysh, x, gate.reshape(1, d)]
    if final:
        in_specs.append(pl.BlockSpec((1, d), lambda i: (0, 0)))
        args.append(final_g.reshape(1, d))
    return pl.pallas_call(
        functools.partial(_combine_kernel, final=final),
        grid=(t // tt,),
        in_specs=in_specs,
        out_specs=pl.BlockSpec((tt, d), lambda i: (i, 0)),
        out_shape=jax.ShapeDtypeStruct((t, d), F32),
        scratch_shapes=[pltpu.SMEM((2, tt * TOP_K), I32),
                        pltpu.VMEM((2, TOP_K, tt, dh), U32),
                        pltpu.SemaphoreType.DMA((2,)),
                        pltpu.SemaphoreType.DMA((2,))],
        compiler_params=_params(("arbitrary",)),
        name="combine",
    )(*args)


def _moe_tables(idx, rank, counts, bm):
    t = idx.shape[0]
    e = N_EXPERTS
    tk = t * TOP_K
    padded = (counts + bm - 1) // bm * bm
    pad_end = jnp.cumsum(padded)
    pad_start = pad_end - padded
    onehot = idx[:, :, None] == jnp.arange(e, dtype=I32)[None, None, :]
    dest = jnp.sum(jnp.where(onehot, pad_start[None, None, :], 0), axis=-1) + rank
    n_blocks = -(-tk // bm) + e
    tok = jnp.broadcast_to(jnp.arange(t, dtype=I32)[:, None], (t, TOP_K))
    row_tok = jnp.zeros((n_blocks * bm,), I32).at[dest.reshape(tk)].set(tok.reshape(tk))
    block_first_row = jnp.arange(n_blocks, dtype=I32) * bm
    block_e = jnp.minimum(jnp.sum(pad_end[None, :] <= block_first_row[:, None], axis=1), e - 1).astype(I32)
    n_used = (pad_end[-1] // bm).astype(I32).reshape(1)
    return dest.astype(I32), row_tok.reshape(n_blocks, bm), block_e, n_used


def _rope_tables(n, n_ctx):
    pos = jnp.arange(n, dtype=I32)
    row = (pos // GRID_W).astype(F32)
    col = (pos % GRID_W).astype(F32)
    n_freq = HEAD_DIM // 4
    inv_freq = ROPE_THETA ** (-jnp.arange(n_freq, dtype=F32) / n_freq)
    ang = jnp.concatenate([row[:, None] * inv_freq, col[:, None] * inv_freq], axis=-1)
    cos = jnp.repeat(jnp.cos(ang), 2, axis=1)
    sin = jnp.repeat(jnp.sin(ang), 2, axis=1)
    even = (jnp.arange(HEAD_DIM) % 2 == 0)[None, :]
    c = jnp.concatenate([jnp.ones((n_ctx, HEAD_DIM), F32), cos], axis=0)
    se = jnp.concatenate([jnp.zeros((n_ctx, HEAD_DIM), F32), jnp.where(even, -sin, 0.0)], axis=0)
    so = jnp.concatenate([jnp.zeros((n_ctx, HEAD_DIM), F32), jnp.where(even, 0.0, sin)], axis=0)
    return c, se, so


def kernel(x, c, ctx, c_ctx, ada_w, ada_b, norm1_g, w_in, q_norm_g, k_norm_g, na_rel_bias, out_norm_g,
           w_out, norm2_g, router_w, router_bias, exp_w_gate, exp_w_up, exp_w_down, shared_w_gate,
           shared_w_up, shared_w_down, final_g):
    b, n, d = x.shape
    l = ctx.shape[1]
    depth = ada_w.shape[0]
    assert b == 1
    xl = x[0]
    xc = ctx[0]
    mod = _adaln(jnp.stack([c_ctx, c[0]], axis=1), ada_w, ada_b)
    tabs_all = _rope_tables(n, l)
    tabs_lat = tuple(t[l:] for t in tabs_all)
    tabs_ctx = tuple(t[:l] for t in tabs_all)

    for layer in range(depth):
        last = layer == depth - 1
        sh1, sc1, g1, sh2, sc2, g2 = [mod[layer, 1, i * d:(i + 1) * d] for i in range(6)]
        csh1, csc1, cg1, csh2, csc2, cg2 = [mod[layer, 0, i * d:(i + 1) * d] for i in range(6)]

        w_in_b = w_in[layer].astype(BF16)
        h = _normmod(xl, norm1_g[layer], sc1, sh1)
        hc = _normmod(xc, norm1_g[layer], csc1, csh1)
        p = _matmul(h, w_in_b, BF16, n_out=F_OFF)
        pc = _matmul(hc, w_in_b, BF16, n_out=F_OFF)
        f = _matmul(h, w_in_b, F32, col_block_off=F_OFF // FOURIER_WIDTH, n_out=FOURIER_WIDTH)

        qa_t = _qprep(p, tabs_lat, q_norm_g[layer])
        ka, va_t = _kvprep(pc, p, tabs_all, k_norm_g[layer])
        o_gqa = _gqa_flash(qa_t, ka, va_t)
        o_na = _na(p, pc, na_rel_bias[layer])
        o_four = _fourier(f)
        w_out_b = w_out[layer].astype(BF16)
        xl = _outproj(o_gqa, o_na, o_four, out_norm_g[layer], w_out_b, xl, g1)

        if not last:
            fc = _matmul(hc, w_in_b, F32, col_block_off=F_OFF // FOURIER_WIDTH, n_out=FOURIER_WIDTH)
            qa_ct = _qprep(pc, tabs_ctx, q_norm_g[layer])
            oc_gqa = _gqa_flash(qa_ct, ka[:l], va_t[:, :l])
            oc_na = _flash(pc, QN_OFF, pc, KN_OFF, pc, VN_OFF, NA_HEADS, 1)
            oc_four = _fourier_small(fc)
            xc = _outproj(oc_gqa, oc_na, oc_four, out_norm_g[layer], w_out_b, xc, cg1)

        if last:
            h2, h2p = _normmod(xl, norm2_g[layer], sc2, sh2, packed=True)
        else:
            h2, h2p = _normmod_pair(xl, xc, norm2_g[layer], sc2, sh2, csc2, csh2)
        idx, wts, rank, counts = _router(h2, router_w[layer].astype(BF16), router_bias[layer])
        dest, row_tok, block_e, n_used = _moe_tables(idx, rank, counts, MOE_ROWS)
        wgu = jnp.concatenate([exp_w_gate[layer], exp_w_up[layer]], axis=-1).astype(BF16)
        ys = _experts(h2p, row_tok, block_e, n_used, wgu, exp_w_down[layer].astype(BF16))
        sgu = jnp.concatenate([shared_w_gate[layer], shared_w_up[layer]], axis=-1).astype(BF16)
        ysh = _shared(h2, sgu, shared_w_down[layer].astype(BF16))
        xl = _combine(dest[:n], wts[:n], ys, ysh, xl, g2, final_g=final_g if last else None)
        if not last:
            xc = _combine(dest[n:], wts[n:], ys, ysh, xc, cg2, ysh_row_off=n)
    return xl[None]
```

```python
import functools
import math

import numpy as np
import jax
import jax.numpy as jnp
from jax import lax
from jax.experimental import pallas as pl
from jax.experimental.pallas import tpu as pltpu

F32 = jnp.float32
BF16 = jnp.bfloat16
U32 = jnp.uint32
I32 = jnp.int32

HEAD_DIM = 128
GRID_W = 64
GQA_HEADS = 16
GQA_KV_HEADS = 4
NA_HEADS = 8
NA_WIN_ROWS = 8
NA_WIN_COLS = 16
FOURIER_GROUPS = 8
GQA_WIDTH = GQA_HEADS * HEAD_DIM
GQA_KV_WIDTH = GQA_KV_HEADS * HEAD_DIM
NA_WIDTH = NA_HEADS * HEAD_DIM
FOURIER_WIDTH = FOURIER_GROUPS * HEAD_DIM
MIX_WIDTH = GQA_WIDTH + NA_WIDTH + FOURIER_WIDTH
IN_WIDTH = GQA_WIDTH + 2 * GQA_KV_WIDTH + 3 * NA_WIDTH + FOURIER_WIDTH
QA_OFF = 0
KA_OFF = GQA_WIDTH
VA_OFF = KA_OFF + GQA_KV_WIDTH
QN_OFF = VA_OFF + GQA_KV_WIDTH
KN_OFF = QN_OFF + NA_WIDTH
VN_OFF = KN_OFF + NA_WIDTH
F_OFF = VN_OFF + NA_WIDTH
ROPE_THETA = 10000.0
N_EXPERTS = 64
TOP_K = 8
N_EXPERT_GROUPS = 8
TOPK_GROUPS = 4
ROUTED_SCALE = 2.5
EPS = 1e-6
LOG2E = 1.4426950408889634
MASK_VALUE = -1e30

VMEM_LIMIT = 56 * 1024 * 1024
MOE_ROWS = 256
EXPERT_K_CHUNKS = 8
COMBINE_TOKENS = 64


def _tile(m, pref, mult=8):
    for t in range(min(pref, m), 0, -1):
        if m % t == 0 and t % mult == 0:
            return t
    return m


def _params(sem):
    return pltpu.CompilerParams(dimension_semantics=sem, vmem_limit_bytes=VMEM_LIMIT)


def _sigmoid(x):
    return 1.0 / (1.0 + jnp.exp(-x))


def _pack_halves(y):
    w = y.shape[1] // 2
    lo = lax.bitcast_convert_type(y[:, :w].astype(BF16).astype(F32), U32) >> 16
    hi = lax.bitcast_convert_type(y[:, w:].astype(BF16).astype(F32), U32) & jnp.uint32(0xFFFF0000)
    return lo | hi


def _unpack_halves(p):
    lo = lax.bitcast_convert_type(p << 16, F32)
    hi = lax.bitcast_convert_type(p & jnp.uint32(0xFFFF0000), F32)
    return lo, hi


def _adaln_kernel(s_ref, w_ref, b_ref, o_ref):
    d = w_ref.shape[1]
    tn = w_ref.shape[2]
    ch = _tile(d, 256)

    def body(r, acc):
        a0, a1 = acc
        r0 = pl.multiple_of(r * ch, ch)
        w = w_ref[0, pl.ds(r0, ch), :]
        s = s_ref[pl.ds(r0, ch), :]
        s = s * _sigmoid(s)
        p0 = (w * s[:, 0:1]).reshape(ch // 8, 8, tn).sum(axis=0)
        p1 = (w * s[:, 1:2]).reshape(ch // 8, 8, tn).sum(axis=0)
        return a0 + p0, a1 + p1

    z = jnp.zeros((8, tn), F32)
    a0, a1 = lax.fori_loop(0, d // ch, body, (z, z))
    o_ref[0, 0:1, :] = a0.sum(axis=0, keepdims=True) + b_ref[0]
    o_ref[0, 1:2, :] = a1.sum(axis=0, keepdims=True) + b_ref[0]


def _adaln(s_cols, ada_w, ada_b):
    depth, d, n6 = ada_w.shape
    tn = _tile(n6, 512, 128)
    return pl.pallas_call(
        _adaln_kernel,
        grid=(depth, n6 // tn),
        in_specs=[
            pl.BlockSpec((d, 2), lambda l, j: (0, 0)),
            pl.BlockSpec((1, d, tn), lambda l, j: (l, 0, j)),
            pl.BlockSpec((1, 1, tn), lambda l, j: (l, 0, j)),
        ],
        out_specs=pl.BlockSpec((1, 2, tn), lambda l, j: (l, 0, j)),
        out_shape=jax.ShapeDtypeStruct((depth, 2, n6), F32),
        compiler_params=_params(("arbitrary", "arbitrary")),
        name="adaln",
    )(s_cols, ada_w, ada_b.reshape(depth, 1, n6))


def _normmod_kernel(x_ref, g_ref, sc_ref, sh_ref, *o_refs):
    x = x_ref[...]
    ms = jnp.mean(x * x, axis=-1, keepdims=True)
    y = x * lax.rsqrt(ms + EPS) * g_ref[...]
    h = y * (1.0 + sc_ref[...]) + sh_ref[...]
    o_refs[0][...] = h.astype(BF16)
    if len(o_refs) > 1:
        o_refs[1][...] = _pack_halves(h)


def _normmod(x, g, sc, sh, packed=False):
    m, d = x.shape
    tm = _tile(m, 256)
    vec = pl.BlockSpec((1, d), lambda i: (0, 0))
    out_shape = [jax.ShapeDtypeStruct((m, d), BF16)]
    out_specs = [pl.BlockSpec((tm, d), lambda i: (i, 0))]
    if packed:
        out_shape.append(jax.ShapeDtypeStruct((m, d // 2), U32))
        out_specs.append(pl.BlockSpec((tm, d // 2), lambda i: (i, 0)))
    res = pl.pallas_call(
        _normmod_kernel,
        grid=(m // tm,),
        in_specs=[pl.BlockSpec((tm, d), lambda i: (i, 0)), vec, vec, vec],
        out_specs=out_specs,
        out_shape=out_shape,
        compiler_params=_params(("arbitrary",)),
        name="normmod",
    )(x, g.reshape(1, d), sc.reshape(1, d), sh.reshape(1, d))
    return res if packed else res[0]


def _normmod_pair_kernel(xa_ref, xb_ref, g_ref, sca_ref, sha_ref, scb_ref, shb_ref, o_ref, op_ref, *, na_blocks):
    i = pl.program_id(0)

    @pl.when(i < na_blocks)
    def _():
        _normmod_kernel(xa_ref, g_ref, sca_ref, sha_ref, o_ref, op_ref)

    @pl.when(i >= na_blocks)
    def _():
        _normmod_kernel(xb_ref, g_ref, scb_ref, shb_ref, o_ref, op_ref)


def _normmod_pair(xa, xb, g, sca, sha, scb, shb):
    ma, d = xa.shape
    mb = xb.shape[0]
    tm = _tile(math.gcd(ma, mb), 256)
    nab = ma // tm
    vec = pl.BlockSpec((1, d), lambda i: (0, 0))
    return pl.pallas_call(
        functools.partial(_normmod_pair_kernel, na_blocks=nab),
        grid=((ma + mb) // tm,),
        in_specs=[pl.BlockSpec((tm, d), lambda i: (jnp.minimum(i, nab - 1), 0)),
                  pl.BlockSpec((tm, d), lambda i: (jnp.maximum(i - nab, 0), 0)),
                  vec, vec, vec, vec, vec],
        out_specs=[pl.BlockSpec((tm, d), lambda i: (i, 0)), pl.BlockSpec((tm, d // 2), lambda i: (i, 0))],
        out_shape=[jax.ShapeDtypeStruct((ma + mb, d), BF16), jax.ShapeDtypeStruct((ma + mb, d // 2), U32)],
        compiler_params=_params(("arbitrary",)),
        name="normmod_pair",
    )(xa, xb, g.reshape(1, d), sca.reshape(1, d), sha.reshape(1, d), scb.reshape(1, d), shb.reshape(1, d))


def _mm_kernel(a_ref, b_ref, o_ref):
    o_ref[...] = jnp.dot(a_ref[...], b_ref[...], preferred_element_type=F32).astype(o_ref.dtype)


def _matmul(a, b, layer, out_dtype, col_block_off=0, n_out=None, tm_pref=1024, tn_pref=1024):
    m, k = a.shape
    n_out = b.shape[2] if n_out is None else n_out
    tm = _tile(m, tm_pref)
    tn = _tile(n_out, tn_pref, 128)
    return pl.pallas_call(
        _mm_kernel,
        grid=(m // tm, n_out // tn),
        in_specs=[
            pl.BlockSpec((tm, k), lambda i, j: (i, 0)),
            pl.BlockSpec((None, k, tn), lambda i, j: (layer, 0, j + col_block_off)),
        ],
        out_specs=pl.BlockSpec((tm, tn), lambda i, j: (i, j)),
        out_shape=jax.ShapeDtypeStruct((m, n_out), out_dtype),
        compiler_params=_params(("arbitrary", "arbitrary")),
        name="matmul",
    )(a, b)


def _head_norm_rope(x, g, c, se, so):
    ms = jnp.mean(x * x, axis=-1, keepdims=True)
    y = x * lax.rsqrt(ms + EPS) * g
    return y * c + pltpu.roll(y, HEAD_DIM - 1, 1) * se + pltpu.roll(y, 1, 1) * so


def _qprep_kernel(p_ref, c_ref, se_ref, so_ref, g_ref, o_ref, *, nheads, scale):
    c, se, so, g = c_ref[...], se_ref[...], so_ref[...], g_ref[...]
    for h in range(nheads):
        sl = slice(h * HEAD_DIM, (h + 1) * HEAD_DIM)
        r = _head_norm_rope(p_ref[:, sl].astype(F32), g, c, se, so) * scale
        o_ref[sl, :] = r.T.astype(o_ref.dtype)


def _qprep(p, tabs, g):
    m = p.shape[0]
    tq = _tile(m, 512, 128)
    tab = pl.BlockSpec((tq, HEAD_DIM), lambda i: (i, 0))
    return pl.pallas_call(
        functools.partial(_qprep_kernel, nheads=GQA_HEADS, scale=HEAD_DIM ** -0.5 * LOG2E),
        grid=(m // tq,),
        in_specs=[pl.BlockSpec((tq, GQA_WIDTH), lambda i: (i, QA_OFF // GQA_WIDTH)), tab, tab, tab,
                  pl.BlockSpec((1, HEAD_DIM), lambda i: (0, 0))],
        out_specs=pl.BlockSpec((GQA_WIDTH, tq), lambda i: (0, i)),
        out_shape=jax.ShapeDtypeStruct((GQA_WIDTH, m), BF16),
        compiler_params=_params(("arbitrary",)),
        name="qprep",
    )(p, *tabs, g.reshape(1, HEAD_DIM))


VT_ROWS = HEAD_DIM + 16


def _kvprep_kernel(pc_ref, pl_ref, c_ref, se_ref, so_ref, g_ref, k_ref, vt_ref, *, n_ctx_blocks):
    i = pl.program_id(0)
    c, se, so, g = c_ref[...], se_ref[...], so_ref[...], g_ref[...]
    t = k_ref.shape[0]
    ones_row = jnp.where(lax.broadcasted_iota(I32, (VT_ROWS - HEAD_DIM, t), 0) == 0, 1.0, 0.0).astype(BF16)

    def run(p_ref):
        for h in range(GQA_KV_HEADS):
            sl = slice(h * HEAD_DIM, (h + 1) * HEAD_DIM)
            k_ref[:, sl] = _head_norm_rope(p_ref[:, sl].astype(F32), g, c, se, so).astype(k_ref.dtype)
            v = p_ref[:, GQA_KV_WIDTH + h * HEAD_DIM:GQA_KV_WIDTH + (h + 1) * HEAD_DIM].astype(F32)
            vt_ref[h * VT_ROWS:h * VT_ROWS + HEAD_DIM, :] = v.T.astype(vt_ref.dtype)
            vt_ref[h * VT_ROWS + HEAD_DIM:(h + 1) * VT_ROWS, :] = ones_row

    @pl.when(i < n_ctx_blocks)
    def _():
        run(pc_ref)

    @pl.when(i >= n_ctx_blocks)
    def _():
        run(pl_ref)


def _kvprep(p_ctx, p_lat, tabs, g):
    l, n = p_ctx.shape[0], p_lat.shape[0]
    t = _tile(math.gcd(l, n), 256, 128)
    ncb = l // t
    w = 2 * GQA_KV_WIDTH
    cb = KA_OFF // w
    tab = pl.BlockSpec((t, HEAD_DIM), lambda i: (i, 0))
    return pl.pallas_call(
        functools.partial(_kvprep_kernel, n_ctx_blocks=ncb),
        grid=((l + n) // t,),
        in_specs=[pl.BlockSpec((t, w), lambda i: (jnp.minimum(i, ncb - 1), cb)),
                  pl.BlockSpec((t, w), lambda i: (jnp.maximum(i - ncb, 0), cb)),
                  tab, tab, tab, pl.BlockSpec((1, HEAD_DIM), lambda i: (0, 0))],
        out_specs=[pl.BlockSpec((t, GQA_KV_WIDTH), lambda i: (i, 0)),
                   pl.BlockSpec((GQA_KV_HEADS * VT_ROWS, t), lambda i: (0, i))],
        out_shape=[jax.ShapeDtypeStruct((l + n, GQA_KV_WIDTH), BF16),
                   jax.ShapeDtypeStruct((GQA_KV_HEADS * VT_ROWS, l + n), BF16)],
        compiler_params=_params(("arbitrary",)),
        name="kvprep",
    )(p_ctx, p_lat, *tabs, g.reshape(1, HEAD_DIM))


def _gqa_flash_kernel(qT_ref, k_ref, vT_ref, o_ref, m_ref, acc_ref, *, group, cb):
    j = pl.program_id(2)
    tq = qT_ref.shape[1]

    @pl.when(j == 0)
    def _():
        m_ref[...] = jnp.full(m_ref.shape, -jnp.inf, F32)
        acc_ref[...] = jnp.zeros(acc_ref.shape, F32)

    k = k_ref[...]
    vT = vT_ref[...]
    nsub = tq // cb
    nblk = group * nsub
    st = {}

    def scores(c):
        h, sub = divmod(c, nsub)
        cols = slice(c * cb, (c + 1) * cb)
        qT = qT_ref[h * HEAD_DIM:(h + 1) * HEAD_DIM, sub * cb:(sub + 1) * cb]
        sT = jnp.dot(k, qT, preferred_element_type=F32)
        m_prev = m_ref[:, cols]
        m_new = jnp.maximum(m_prev, jnp.max(sT, axis=0, keepdims=True))
        m_ref[:, cols] = m_new
        st[c] = (sT, m_new, jnp.exp2(m_prev - m_new))

    def probs(c):
        sT, m_new, alpha = st[c]
        st[c] = (jnp.exp2(sT - m_new).astype(BF16), alpha)

    def accumulate(c):
        cols = slice(c * cb, (c + 1) * cb)
        pT, alpha = st.pop(c)
        acc_ref[:, cols] = alpha * acc_ref[:, cols] + jnp.dot(vT, pT, preferred_element_type=F32)

    scores(0)
    for c in range(nblk):
        probs(c)
        if c + 1 < nblk:
            scores(c + 1)
        accumulate(c)

    @pl.when(j == pl.num_programs(2) - 1)
    def _():
        for h in range(group):
            cols = slice(h * tq, (h + 1) * tq)
            o = acc_ref[:HEAD_DIM, cols] * (1.0 / acc_ref[HEAD_DIM:HEAD_DIM + 1, cols])
            o_ref[:, h * HEAD_DIM:(h + 1) * HEAD_DIM] = o.T.astype(o_ref.dtype)


def _gqa_flash(qT, k, vT, tq_pref=1024, tk_pref=3328, cb=512):
    nq, tk_total = qT.shape[1], k.shape[0]
    group = GQA_HEADS // GQA_KV_HEADS
    tq = _tile(nq, tq_pref, 128)
    tk = _tile(tk_total, tk_pref, 128)
    gw = group * HEAD_DIM
    return pl.pallas_call(
        functools.partial(_gqa_flash_kernel, group=group, cb=min(cb, tq)),
        grid=(GQA_KV_HEADS, nq // tq, tk_total // tk),
        in_specs=[
            pl.BlockSpec((gw, tq), lambda g, i, j: (g, i)),
            pl.BlockSpec((tk, HEAD_DIM), lambda g, i, j: (j, g)),
            pl.BlockSpec((VT_ROWS, tk), lambda g, i, j: (g, j)),
        ],
        out_specs=pl.BlockSpec((tq, gw), lambda g, i, j: (i, g)),
        out_shape=jax.ShapeDtypeStruct((nq, GQA_WIDTH), BF16),
        scratch_shapes=[
            pltpu.VMEM((1, group * tq), F32),
            pltpu.VMEM((VT_ROWS, group * tq), F32),
        ],
        compiler_params=_params(("arbitrary", "arbitrary", "arbitrary")),
        name="gqa_flash",
    )(qT, k, vT)


def _flash_kernel(q_ref, k_ref, v_ref, o_ref, qs_ref, m_ref, l_ref, acc_ref, *, group, scale):
    j = pl.program_id(2)
    tq = q_ref.shape[0]

    @pl.when(j == 0)
    def _():
        for h in range(group):
            sl = slice(h * HEAD_DIM, (h + 1) * HEAD_DIM)
            qs_ref[h * tq:(h + 1) * tq, :] = (q_ref[:, sl].astype(F32) * scale).astype(BF16)
        m_ref[...] = jnp.full(m_ref.shape, -jnp.inf, F32)
        l_ref[...] = jnp.zeros(l_ref.shape, F32)
        acc_ref[...] = jnp.zeros(acc_ref.shape, F32)

    s = lax.dot_general(qs_ref[...], k_ref[...], (((1,), (1,)), ((), ())), preferred_element_type=F32)
    m_prev = m_ref[...]
    m_new = jnp.maximum(m_prev, jnp.max(s, axis=-1, keepdims=True))
    alpha = jnp.exp2(m_prev - m_new)
    p = jnp.exp2(s - m_new)
    l_ref[...] = alpha * l_ref[...] + jnp.sum(p, axis=-1, keepdims=True)
    acc_ref[...] = alpha * acc_ref[...] + jnp.dot(p.astype(BF16), v_ref[...], preferred_element_type=F32)
    m_ref[...] = m_new

    @pl.when(j == pl.num_programs(2) - 1)
    def _():
        inv = 1.0 / l_ref[...]
        for h in range(group):
            sl = slice(h * HEAD_DIM, (h + 1) * HEAD_DIM)
            o_ref[:, sl] = (acc_ref[h * tq:(h + 1) * tq, :] * inv[h * tq:(h + 1) * tq]).astype(o_ref.dtype)


def _flash(q, q_off, k, k_off, v, v_off, n_kv_heads, group, tq_pref=512, tk_pref=1280):
    nq, tk_total = q.shape[0], k.shape[0]
    tq = _tile(nq, tq_pref)
    tk = _tile(tk_total, tk_pref, 128)
    gw = group * HEAD_DIM
    scale = HEAD_DIM ** -0.5 * LOG2E
    return pl.pallas_call(
        functools.partial(_flash_kernel, group=group, scale=scale),
        grid=(n_kv_heads, nq // tq, tk_total // tk),
        in_specs=[
            pl.BlockSpec((tq, gw), lambda g, i, j: (i, q_off // gw + g)),
            pl.BlockSpec((tk, HEAD_DIM), lambda g, i, j: (j, k_off // HEAD_DIM + g)),
            pl.BlockSpec((tk, HEAD_DIM), lambda g, i, j: (j, v_off // HEAD_DIM + g)),
        ],
        out_specs=pl.BlockSpec((tq, gw), lambda g, i, j: (i, g)),
        out_shape=jax.ShapeDtypeStruct((nq, n_kv_heads * gw), BF16),
        scratch_shapes=[
            pltpu.VMEM((group * tq, HEAD_DIM), BF16),
            pltpu.VMEM((group * tq, 1), F32),
            pltpu.VMEM((group * tq, 1), F32),
            pltpu.VMEM((group * tq, HEAD_DIM), F32),
        ],
        compiler_params=_params(("arbitrary", "arbitrary", "arbitrary")),
        name="flash",
    )(q, k, v)


NA_QROWS = 8
NA_BAND = 16


def _na_kernel(q_ref, k0, k1, k2, k3, v0, v1, v2, v3, kc_ref, vc_ref, tb_ref, o_ref, kb_ref, vb_ref,
               *, rows):
    b = pl.program_id(0)
    quarter = NA_BAND * GRID_W // 4
    for d, (kr, vr) in enumerate(((k0, v0), (k1, v1), (k2, v2), (k3, v3))):
        kb_ref[d * quarter:(d + 1) * quarter, :] = kr[...]
        vb_ref[d * quarter:(d + 1) * quarter, :] = vr[...]
    band_start = jnp.clip(NA_QROWS * b - NA_WIN_ROWS // 2, 0, rows - NA_BAND)
    scale = HEAD_DIM ** -0.5
    win = NA_WIN_ROWS * GRID_W

    def body(a, carry):
        r = NA_QROWS * b + a
        r_start = jnp.clip(r - NA_WIN_ROWS // 2, 0, rows - NA_WIN_ROWS)
        dd = r - r_start
        koff = pl.multiple_of((r_start - band_start) * GRID_W, GRID_W)
        qoff = pl.multiple_of(a * GRID_W, GRID_W)
        q_a = q_ref[pl.ds(qoff, GRID_W), :]
        kw = kb_ref[pl.ds(koff, win), :]
        vw = vb_ref[pl.ds(koff, win), :]
        for h in range(NA_HEADS):
            sl = slice(h * HEAD_DIM, (h + 1) * HEAD_DIM)
            qh = q_a[:, sl]
            dn = (((1,), (1,)), ((), ()))
            s_nb = lax.dot_general(qh, kw[:, sl], dn, preferred_element_type=F32) * scale + tb_ref[h, dd]
            s_c = lax.dot_general(qh, kc_ref[:, sl], dn, preferred_element_type=F32) * scale
            m = jnp.maximum(jnp.max(s_nb, axis=-1, keepdims=True), jnp.max(s_c, axis=-1, keepdims=True))
            p_nb = jnp.exp(s_nb - m)
            p_c = jnp.exp(s_c - m)
            l = jnp.sum(p_nb, axis=-1, keepdims=True) + jnp.sum(p_c, axis=-1, keepdims=True)
            o = (jnp.dot(p_c.astype(BF16), vc_ref[:, sl], preferred_element_type=F32)
                 + jnp.dot(p_nb.astype(BF16), vw[:, sl], preferred_element_type=F32))
            o_ref[pl.ds(qoff, GRID_W), sl] = (o / l).astype(o_ref.dtype)
        return carry

    lax.fori_loop(0, NA_QROWS, body, 0)


def _na_bias_table(rel_bias):
    h = rel_bias.shape[0]
    wr, wc = NA_WIN_ROWS, NA_WIN_COLS
    nro, nco = 2 * wr - 1, 2 * wc - 1
    p = 2 * GRID_W
    w = jnp.full((h, nro, p), MASK_VALUE, F32).at[:, :, :nco].set(rel_bias.astype(F32))
    sk = jnp.broadcast_to(w[:, :, None, :], (h, nro, GRID_W, p)).reshape(h, nro, GRID_W * p)
    sk = sk[:, :, :GRID_W * (p - 1)].reshape(h, nro, GRID_W, p - 1)[:, :, :, wc - 1:wc - 1 + GRID_W]
    qc = np.arange(GRID_W)
    kc = np.arange(GRID_W)
    col_start = np.clip(qc - wc // 2, 0, GRID_W - wc)
    valid = (kc[None, :] >= col_start[:, None]) & (kc[None, :] < col_start[:, None] + wc)
    sk = jnp.where(jnp.asarray(valid)[None, None], sk, MASK_VALUE)
    tb = jnp.stack([sk[:, wr - 1 - dd:2 * wr - 1 - dd] for dd in range(wr)], axis=1)
    return tb.transpose(0, 1, 3, 2, 4).reshape(h, wr, GRID_W, wr * GRID_W)


def _na(p_lat, p_ctx, rel_bias):
    n, l = p_lat.shape[0], p_ctx.shape[0]
    rows = n // GRID_W
    assert rows % NA_QROWS == 0 and rows >= NA_BAND
    tq = NA_QROWS * GRID_W
    quarter = NA_BAND * GRID_W // 4
    nquart = n // quarter

    def band_block(d, col):
        return pl.BlockSpec((quarter, NA_WIDTH),
                            lambda b: (jnp.clip(2 * b - 1, 0, nquart - 4) + d, col))

    kcol, vcol = KN_OFF // NA_WIDTH, VN_OFF // NA_WIDTH
    tb = _na_bias_table(rel_bias)
    return pl.pallas_call(
        functools.partial(_na_kernel, rows=rows),
        grid=(rows // NA_QROWS,),
        in_specs=[pl.BlockSpec((tq, NA_WIDTH), lambda b: (b, QN_OFF // NA_WIDTH))]
        + [band_block(d, kcol) for d in range(4)]
        + [band_block(d, vcol) for d in range(4)]
        + [pl.BlockSpec((l, NA_WIDTH), lambda b: (0, kcol)),
           pl.BlockSpec((l, NA_WIDTH), lambda b: (0, vcol)),
           pl.BlockSpec(tb.shape, lambda b: (0, 0, 0, 0))],
        out_specs=pl.BlockSpec((tq, NA_WIDTH), lambda b: (b, 0)),
        out_shape=jax.ShapeDtypeStruct((n, NA_WIDTH), BF16),
        scratch_shapes=[pltpu.VMEM((NA_BAND * GRID_W, NA_WIDTH), BF16),
                        pltpu.VMEM((NA_BAND * GRID_W, NA_WIDTH), BF16)],
        compiler_params=_params(("arbitrary",)),
        name="natten",
    )(p_lat, *([p_lat] * 8), p_ctx, p_ctx, tb)


def _dft_cs(n):
    k = np.arange(n, dtype=np.float64)
    ang = 2.0 * np.pi * np.outer(k, k) / n
    return np.cos(ang), np.sin(ang)


def _f1_kernel(x_ref, da_ref, db_ref, twr_ref, twi_ref, u_ref, *, tb, n_hi):
    w = FOURIER_WIDTH
    for j in range(tb):
        x = x_ref[:, j, :].astype(BF16)
        zs = [jnp.dot(x[:, g * HEAD_DIM:(g + 1) * HEAD_DIM], da_ref[...], preferred_element_type=F32)
              for g in range(FOURIER_GROUPS)]
        zr = jnp.concatenate([z[:, :HEAD_DIM] for z in zs], axis=1)
        zi = jnp.concatenate([z[:, HEAD_DIM:] for z in zs], axis=1)
        zst = jnp.concatenate([zr, zi], axis=0).astype(BF16)
        u = jnp.dot(db_ref[...], zst, preferred_element_type=F32)
        ur, ui = u[:n_hi], u[n_hi:]
        tr, ti = twr_ref[j], twi_ref[j]
        u_ref[:, j, :w] = ur * tr - ui * ti
        u_ref[:, j, w:] = ur * ti + ui * tr


def _f2_kernel(u_ref, dc_ref, y_ref, *, ta):
    w = FOURIER_WIDTH
    for j in range(ta):
        u = u_ref[j]
        ust = jnp.concatenate([u[:, :w], u[:, w:]], axis=0).astype(BF16)
        y_ref[:, j, :] = jnp.dot(dc_ref[...], ust, preferred_element_type=F32)


def _fourier(xf):
    n, w = xf.shape
    n_lo = 128
    n_hi = n // n_lo
    assert n_hi * n_lo == n and n_hi % 8 == 0
    cc, sc = _dft_cs(HEAD_DIM)
    da = jnp.asarray(np.concatenate([cc, -sc], axis=1), BF16)
    cn, sn = _dft_cs(n_hi)
    db = jnp.asarray(np.block([[cn, sn], [-sn, cn]]), BF16)
    ang = 2.0 * np.pi * np.outer(np.arange(n_lo), np.arange(n_hi)) / n
    twr = jnp.asarray(np.cos(ang)[:, :, None], F32)
    twi = jnp.asarray(-np.sin(ang)[:, :, None], F32)
    cl, sl = _dft_cs(n_lo)
    dc = jnp.asarray(np.concatenate([cl, sl], axis=1), BF16)
    tb = 8
    u = pl.pallas_call(
        functools.partial(_f1_kernel, tb=tb, n_hi=n_hi),
        grid=(n_lo // tb,),
        in_specs=[pl.BlockSpec((n_hi, tb, w), lambda j: (0, j, 0)),
                  pl.BlockSpec(da.shape, lambda j: (0, 0)),
                  pl.BlockSpec(db.shape, lambda j: (0, 0)),
                  pl.BlockSpec((tb, n_hi, 1), lambda j: (j, 0, 0)),
                  pl.BlockSpec((tb, n_hi, 1), lambda j: (j, 0, 0))],
        out_specs=pl.BlockSpec((n_hi, tb, 2 * w), lambda j: (0, j, 0)),
        out_shape=jax.ShapeDtypeStruct((n_hi, n_lo, 2 * w), F32),
        compiler_params=_params(("arbitrary",)),
        name="fourier_stage1",
    )(xf.reshape(n_hi, n_lo, w), da, db, twr, twi)
    ta = 8
    y = pl.pallas_call(
        functools.partial(_f2_kernel, ta=ta),
        grid=(n_hi // ta,),
        in_specs=[pl.BlockSpec((ta, n_lo, 2 * w), lambda i: (i, 0, 0)),
                  pl.BlockSpec(dc.shape, lambda i: (0, 0))],
        out_specs=pl.BlockSpec((n_lo, ta, w), lambda i: (0, i, 0)),
        out_shape=jax.ShapeDtypeStruct((n_lo, n_hi, w), F32),
        compiler_params=_params(("arbitrary",)),
        name="fourier_stage2",
    )(u, dc)
    return y.reshape(n, w)


def _fourier_small_kernel(x_ref, dch_ref, dpos_ref, y_ref):
    x = x_ref[...].astype(BF16)
    for g in range(FOURIER_GROUPS):
        sl = slice(g * HEAD_DIM, (g + 1) * HEAD_DIM)
        a = jnp.dot(x[:, sl], dch_ref[...], preferred_element_type=F32)
        ast = jnp.concatenate([a[:, :HEAD_DIM], a[:, HEAD_DIM:]], axis=0).astype(BF16)
        y_ref[:, sl] = jnp.dot(dpos_ref[...], ast, preferred_element_type=F32)


def _fourier_small(xf):
    t, w = xf.shape
    cc, sc = _dft_cs(HEAD_DIM)
    dch = jnp.asarray(np.concatenate([cc, sc], axis=1), BF16)
    cp, sp = _dft_cs(t)
    dpos = jnp.asarray(np.concatenate([cp, -sp], axis=1), BF16)
    return pl.pallas_call(
        _fourier_small_kernel,
        grid=(1,),
        in_specs=[pl.BlockSpec((t, w), lambda i: (0, 0)),
                  pl.BlockSpec(dch.shape, lambda i: (0, 0)),
                  pl.BlockSpec(dpos.shape, lambda i: (0, 0))],
        out_specs=pl.BlockSpec((t, w), lambda i: (0, 0)),
        out_shape=jax.ShapeDtypeStruct((t, w), F32),
        compiler_params=_params(("arbitrary",)),
        name="fourier_small",
    )(xf, dch, dpos)


def _outproj_kernel(og_ref, on_ref, of_ref, g_ref, w_ref, x_ref, gate_ref, o_ref, m_ref):
    @pl.when(pl.program_id(1) == 0)
    def _():
        def nrm(v, g):
            vf = v.astype(F32)
            return (vf * lax.rsqrt(jnp.mean(vf * vf, axis=-1, keepdims=True) + EPS) * g).astype(BF16)

        a, b = GQA_WIDTH, GQA_WIDTH + NA_WIDTH
        m_ref[:, :a] = nrm(og_ref[...], g_ref[:, :a])
        m_ref[:, a:b] = nrm(on_ref[...], g_ref[:, a:b])
        m_ref[:, b:] = nrm(of_ref[...], g_ref[:, b:])

    acc = jnp.dot(m_ref[...], w_ref[...], preferred_element_type=F32)
    o_ref[...] = x_ref[...] + gate_ref[...] * acc


def _outproj(o_gqa, o_na, o_four, g_out, w_out, layer, x, gate):
    m, d = x.shape
    tm = _tile(m, 512)
    tn = _tile(d, 1024, 128)
    return pl.pallas_call(
        _outproj_kernel,
        grid=(m // tm, d // tn),
        in_specs=[pl.BlockSpec((tm, GQA_WIDTH), lambda i, j: (i, 0)),
                  pl.BlockSpec((tm, NA_WIDTH), lambda i, j: (i, 0)),
                  pl.BlockSpec((tm, FOURIER_WIDTH), lambda i, j: (i, 0)),
                  pl.BlockSpec((1, MIX_WIDTH), lambda i, j: (0, 0)),
                  pl.BlockSpec((None, MIX_WIDTH, tn), lambda i, j: (layer, 0, j)),
                  pl.BlockSpec((tm, tn), lambda i, j: (i, j)),
                  pl.BlockSpec((1, tn), lambda i, j: (0, j))],
        out_specs=pl.BlockSpec((tm, tn), lambda i, j: (i, j)),
        out_shape=jax.ShapeDtypeStruct((m, d), F32),
        scratch_shapes=[pltpu.VMEM((tm, MIX_WIDTH), BF16)],
        compiler_params=_params(("arbitrary", "arbitrary")),
        name="outproj",
    )(o_gqa, o_na, o_four, g_out.reshape(1, MIX_WIDTH), w_out, x, gate.reshape(1, d))


def _router_kernel(h_ref, rw_ref, rb_ref, idx_ref, wt_ref, rank_ref, cnt_ref, carry_ref):
    step = pl.program_id(0)

    @pl.when(step == 0)
    def _():
        carry_ref[...] = jnp.zeros(carry_ref.shape, F32)

    tm = h_ref.shape[0]
    e = N_EXPERTS
    gsz = e // N_EXPERT_GROUPS
    neg = -jnp.inf
    logits = jnp.dot(h_ref[...], rw_ref[...], preferred_element_type=F32)
    scores = _sigmoid(logits)
    biased = scores + rb_ref[...]
    lane = lax.broadcasted_iota(I32, (tm, e), 1).astype(F32)
    lane_grp = lax.broadcasted_iota(I32, (tm, e), 1) // gsz

    def first_argmax(v):
        m = jnp.max(v, axis=-1, keepdims=True)
        return m, jnp.min(jnp.where(v == m, lane, float(e)), axis=-1, keepdims=True)

    gs = []
    for g in range(N_EXPERT_GROUPS):
        vg = jnp.where(lane_grp == g, biased, neg)
        m1, i1 = first_argmax(vg)
        m2 = jnp.max(jnp.where(lane == i1, neg, vg), axis=-1, keepdims=True)
        gs.append(m1 + m2)
    masked = jnp.full((tm, e), neg, F32)
    for g in range(N_EXPERT_GROUPS):
        ahead = jnp.zeros((tm, 1), F32)
        for g2 in range(N_EXPERT_GROUPS):
            if g2 == g:
                continue
            better = (gs[g2] > gs[g]) | ((gs[g2] == gs[g]) & (g2 < g))
            ahead = ahead + jnp.where(better, 1.0, 0.0)
        masked = jnp.where((lane_grp == g) & (ahead < TOPK_GROUPS), biased, masked)
    sel = jnp.zeros((tm, e), F32)
    idxs, ws = [], []
    cur = masked
    for _ in range(TOP_K):
        _, ik = first_argmax(cur)
        hit = lane == ik
        ws.append(jnp.sum(jnp.where(hit, scores, 0.0), axis=-1, keepdims=True))
        cur = jnp.where(hit, neg, cur)
        sel = sel + jnp.where(hit, 1.0, 0.0)
        idxs.append(ik)
    wsum = ws[0]
    for wk in ws[1:]:
        wsum = wsum + wk
    rr = lax.broadcasted_iota(I32, (tm, tm), 0)
    cc = lax.broadcasted_iota(I32, (tm, tm), 1)
    tri = jnp.where(cc < rr, 1.0, 0.0).astype(BF16)
    pref = jnp.dot(tri, sel.astype(BF16), preferred_element_type=F32) + carry_ref[...]
    ranks = [jnp.sum(jnp.where(lane == ik, pref, 0.0), axis=-1, keepdims=True) for ik in idxs]
    carry_ref[...] = carry_ref[...] + jnp.sum(sel, axis=0, keepdims=True)

    lane_o = lax.broadcasted_iota(I32, (tm, 128), 1)
    o_idx = jnp.zeros((tm, 128), F32)
    o_w = jnp.zeros((tm, 128), F32)
    o_rank = jnp.zeros((tm, 128), F32)
    for k in range(TOP_K):
        o_idx = jnp.where(lane_o == k, idxs[k], o_idx)
        o_w = jnp.where(lane_o == k, ws[k] / wsum * ROUTED_SCALE, o_w)
        o_rank = jnp.where(lane_o == k, ranks[k], o_rank)
    idx_ref[...] = o_idx.astype(I32)
    wt_ref[...] = o_w
    rank_ref[...] = o_rank.astype(I32)
    cnt_ref[...] = jnp.broadcast_to(carry_ref[...], cnt_ref.shape)


def _router(h, rw, rb):
    t, d = h.shape
    tm = _tile(t, 256)
    blk = pl.BlockSpec((tm, 128), lambda i: (i, 0))
    idx, wt, rank, cnt = pl.pallas_call(
        _router_kernel,
        grid=(t // tm,),
        in_specs=[pl.BlockSpec((tm, d), lambda i: (i, 0)),
                  pl.BlockSpec((d, N_EXPERTS), lambda i: (0, 0)),
                  pl.BlockSpec((1, N_EXPERTS), lambda i: (0, 0))],
        out_specs=[blk, blk, blk, pl.BlockSpec((8, N_EXPERTS), lambda i: (0, 0))],
        out_shape=[jax.ShapeDtypeStruct((t, 128), I32), jax.ShapeDtypeStruct((t, 128), F32),
                   jax.ShapeDtypeStruct((t, 128), I32), jax.ShapeDtypeStruct((8, N_EXPERTS), F32)],
        scratch_shapes=[pltpu.VMEM((1, N_EXPERTS), F32)],
        compiler_params=_params(("arbitrary",)),
        name="router",
    )(h, rw, rb.reshape(1, N_EXPERTS))
    return idx[:, :TOP_K], wt[:, :TOP_K], rank[:, :TOP_K], cnt[0].astype(I32)


def _expert_kernel(be_ref, nu_ref, rt_hbm, hp_hbm, wgu_ref, wd_ref, o_ref, idx_smem, xbuf, idx_sem, x_sem):
    b = pl.program_id(0)
    n_used = nu_ref[0]
    n_blocks = pl.num_programs(0)
    bm = xbuf.shape[1]
    dh = xbuf.shape[2]
    ff = wd_ref.shape[1]
    slot = b % 2
    nslot = 1 - slot
    n_chunks = max(1, min(EXPERT_K_CHUNKS, dh // 128))
    kc = dh // n_chunks
    issue_chunks = max(1, n_chunks // 2)
    per = bm // issue_chunks

    def idx_copy(s, sl):
        return pltpu.make_async_copy(rt_hbm.at[s], idx_smem.at[sl], idx_sem.at[sl])

    def row_copy(sl, r, tok):
        return pltpu.make_async_copy(hp_hbm.at[pl.ds(tok, 1), :], xbuf.at[sl, pl.ds(r, 1), :], x_sem.at[sl])

    def slot_wait(sl):
        pltpu.make_async_copy(xbuf.at[sl], xbuf.at[sl], x_sem.at[sl]).wait()

    @pl.when(b == 0)
    def _():
        idx_copy(0, 0).start()
        idx_copy(0, 0).wait()

        def body(r, carry):
            row_copy(0, r, idx_smem[0, r]).start()
            return carry

        lax.fori_loop(0, bm, body, 0)
        idx_copy(1, 1).start()

    @pl.when(b < n_used)
    def _():
        idx_copy(b + 1, nslot).wait()
        slot_wait(slot)
        h = None
        for c in range(n_chunks):
            lo, hi = _unpack_halves(xbuf[slot, :, c * kc:(c + 1) * kc])
            part = (jnp.dot(lo.astype(BF16), wgu_ref[0, c * kc:(c + 1) * kc, :], preferred_element_type=F32)
                    + jnp.dot(hi.astype(BF16), wgu_ref[0, dh + c * kc:dh + (c + 1) * kc, :],
                              preferred_element_type=F32))
            h = part if h is None else h + part
            if c < issue_chunks:
                for r in range(c * per, (c + 1) * per):
                    row_copy(nslot, r, idx_smem[nslot, r]).start()
        idx_copy(jnp.minimum(b + 2, n_blocks - 1), slot).start()
        gate, up = h[:, :ff], h[:, ff:]
        act = (gate * _sigmoid(gate) * up).astype(BF16)
        y = jnp.dot(act, wd_ref[0], preferred_element_type=F32)
        o_ref[...] = _pack_halves(y)

    @pl.when(b >= n_used)
    def _():
        o_ref[...] = jnp.zeros(o_ref.shape, o_ref.dtype)

    @pl.when(b == n_used)
    def _():
        idx_copy(0, nslot).wait()
        slot_wait(slot)


def _experts(hp, row_tok, block_e, n_used, wgu, wd, layer):
    n_blocks, bm = row_tok.shape
    dh = hp.shape[1]
    d = 2 * dh
    ff = wd.shape[2]
    grid_spec = pltpu.PrefetchScalarGridSpec(
        num_scalar_prefetch=2,
        grid=(n_blocks,),
        in_specs=[pl.BlockSpec(memory_space=pl.ANY),
                  pl.BlockSpec(memory_space=pl.ANY),
                  pl.BlockSpec((None, 1, d, 2 * ff), lambda b, be, nu: (layer, be[b], 0, 0)),
                  pl.BlockSpec((None, 1, ff, d), lambda b, be, nu: (layer, be[b], 0, 0))],
        out_specs=pl.BlockSpec((bm, dh), lambda b, be, nu: (b, 0)),
        scratch_shapes=[pltpu.SMEM((2, bm), I32),
                        pltpu.VMEM((2, bm, dh), U32),
                        pltpu.SemaphoreType.DMA((2,)),
                        pltpu.SemaphoreType.DMA((2,))],
    )
    return pl.pallas_call(
        _expert_kernel,
        grid_spec=grid_spec,
        out_shape=jax.ShapeDtypeStruct((n_blocks * bm, dh), U32),
        compiler_params=_params(("arbitrary",)),
        name="experts",
    )(block_e, n_used, row_tok, hp, wgu, wd)


def _shared_kernel(h_ref, wgu_ref, wd_ref, o_ref):
    ff = wd_ref.shape[0]
    h = jnp.dot(h_ref[...], wgu_ref[...], preferred_element_type=F32)
    gate, up = h[:, :ff], h[:, ff:]
    act = (gate * _sigmoid(gate) * up).astype(BF16)
    o_ref[...] = jnp.dot(act, wd_ref[...], preferred_element_type=F32).astype(o_ref.dtype)


def _shared(h, wgu, wd, layer):
    t, d = h.shape
    ff = wd.shape[1]
    tm = _tile(t, 512)
    return pl.pallas_call(
        _shared_kernel,
        grid=(t // tm,),
        in_specs=[pl.BlockSpec((tm, d), lambda i: (i, 0)),
                  pl.BlockSpec((None, d, 2 * ff), lambda i: (layer, 0, 0)),
                  pl.BlockSpec((None, ff, d), lambda i: (layer, 0, 0))],
        out_specs=pl.BlockSpec((tm, d), lambda i: (i, 0)),
        out_shape=jax.ShapeDtypeStruct((t, d), BF16),
        compiler_params=_params(("arbitrary",)),
        name="shared_expert",
    )(h, wgu, wd)


def _combine_kernel(dest_hbm, ys_hbm, w_ref, ysh_ref, x_ref, gate_ref, *rest, final):
    if final:
        fg_ref, o_ref, idx_smem, gbuf, idx_sem, g_sem = rest
    else:
        o_ref, idx_smem, gbuf, idx_sem, g_sem = rest
    s = pl.program_id(0)
    last = pl.num_programs(0) - 1
    tt = gbuf.shape[2]
    dh = gbuf.shape[3]
    slot = s % 2
    nslot = 1 - slot
    per = tt // TOP_K

    def idx_copy(st, sl):
        return pltpu.make_async_copy(dest_hbm.at[st], idx_smem.at[sl], idx_sem.at[sl])

    def row_copy(sl, k, t, row):
        return pltpu.make_async_copy(ys_hbm.at[pl.ds(row, 1), :], gbuf.at[sl, k, pl.ds(t, 1), :], g_sem.at[sl])

    def slot_wait(sl):
        pltpu.make_async_copy(gbuf.at[sl], gbuf.at[sl], g_sem.at[sl]).wait()

    @pl.when(s == 0)
    def _():
        idx_copy(0, 0).start()
        idx_copy(0, 0).wait()

        def body(t, carry):
            for k in range(TOP_K):
                row_copy(0, k, t, idx_smem[0, t * TOP_K + k]).start()
            return carry

        lax.fori_loop(0, tt, body, 0)
        idx_copy(jnp.minimum(1, last), 1).start()

    idx_copy(0, nslot).wait()
    slot_wait(slot)
    w = w_ref[...]
    acc_lo = jnp.zeros((tt, dh), F32)
    acc_hi = jnp.zeros((tt, dh), F32)
    for k in range(TOP_K):
        lo, hi = _unpack_halves(gbuf[slot, k])
        wk = w[:, k:k + 1]
        acc_lo = acc_lo + wk * lo
        acc_hi = acc_hi + wk * hi
        for t in range(k * per, (k + 1) * per):
            for kk in range(TOP_K):
                row_copy(nslot, kk, t, idx_smem[nslot, t * TOP_K + kk]).start()
    idx_copy(jnp.minimum(s + 2, last), slot).start()
    y = jnp.concatenate([acc_lo, acc_hi], axis=1) + ysh_ref[...].astype(F32)
    xn = x_ref[...] + gate_ref[...] * y
    if final:
        xn = xn * lax.rsqrt(jnp.mean(xn * xn, axis=-1, keepdims=True) + EPS) * fg_ref[...]
    o_ref[...] = xn

    @pl.when(s == last)
    def _():
        idx_copy(0, slot).wait()
        slot_wait(nslot)


def _combine(dest, w, ys, ysh, x, gate, final_g=None, ysh_row_off=0):
    t, d = x.shape
    tt = _tile(t, COMBINE_TOKENS)
    dh = d // 2
    final = final_g is not None
    off_blocks, rem = divmod(ysh_row_off, tt)
    assert rem == 0
    in_specs = [pl.BlockSpec(memory_space=pl.ANY),
                pl.BlockSpec(memory_space=pl.ANY),
                pl.BlockSpec((tt, TOP_K), lambda i: (i, 0)),
                pl.BlockSpec((tt, d), lambda i: (i + off_blocks, 0)),
                pl.BlockSpec((tt, d), lambda i: (i, 0)),
                pl.BlockSpec((1, d), lambda i: (0, 0))]
    args = [dest.reshape(t // tt, tt * TOP_K), ys, w, ysh, x, gate.reshape(1, d)]
    if final:
        in_specs.append(pl.BlockSpec((1, d), lambda i: (0, 0)))
        args.append(final_g.reshape(1, d))
    return pl.pallas_call(
        functools.partial(_combine_kernel, final=final),
        grid=(t // tt,),
        in_specs=in_specs,
        out_specs=pl.BlockSpec((tt, d), lambda i: (i, 0)),
        out_shape=jax.ShapeDtypeStruct((t, d), F32),
        scratch_shapes=[pltpu.SMEM((2, tt * TOP_K), I32),
                        pltpu.VMEM((2, TOP_K, tt, dh), U32),
                        pltpu.SemaphoreType.DMA((2,)),
                        pltpu.SemaphoreType.DMA((2,))],
        compiler_params=_params(("arbitrary",)),
        name="combine",
    )(*args)


def _moe_tables(idx, rank, counts, bm):
    t = idx.shape[0]
    e = N_EXPERTS
    tk = t * TOP_K
    padded = (counts + bm - 1) // bm * bm
    pad_end = jnp.cumsum(padded)
    pad_start = pad_end - padded
    onehot = idx[:, :, None] == jnp.arange(e, dtype=I32)[None, None, :]
    dest = jnp.sum(jnp.where(onehot, pad_start[None, None, :], 0), axis=-1) + rank
    n_blocks = -(-tk // bm) + e
    tok = jnp.broadcast_to(jnp.arange(t, dtype=I32)[:, None], (t, TOP_K))
    row_tok = jnp.zeros((n_blocks * bm,), I32).at[dest.reshape(tk)].set(tok.reshape(tk))
    block_first_row = jnp.arange(n_blocks, dtype=I32) * bm
    block_e = jnp.minimum(jnp.sum(pad_end[None, :] <= block_first_row[:, None], axis=1), e - 1).astype(I32)
    n_used = (pad_end[-1] // bm).astype(I32).reshape(1)
    return dest.astype(I32), row_tok.reshape(n_blocks, bm), block_e, n_used


def _rope_tables(n, n_ctx):
    rows = n // GRID_W
    n_freq = HEAD_DIM // 4
    inv_freq = ROPE_THETA ** (-jnp.arange(n_freq, dtype=F32) / n_freq)
    ang_r = jnp.arange(rows, dtype=F32)[:, None] * inv_freq
    ang_c = jnp.arange(GRID_W, dtype=F32)[:, None] * inv_freq

    def expand(fn):
        r = jnp.broadcast_to(fn(ang_r)[:, None, :], (rows, GRID_W, n_freq))
        c_ = jnp.broadcast_to(fn(ang_c)[None, :, :], (rows, GRID_W, n_freq))
        half = jnp.concatenate([r, c_], axis=-1).reshape(n, 2 * n_freq)
        return jnp.repeat(half, 2, axis=1)

    cos = expand(jnp.cos)
    sin = expand(jnp.sin)
    even = (jnp.arange(HEAD_DIM) % 2 == 0)[None, :]
    c = jnp.concatenate([jnp.ones((n_ctx, HEAD_DIM), F32), cos], axis=0)
    se = jnp.concatenate([jnp.zeros((n_ctx, HEAD_DIM), F32), jnp.where(even, -sin, 0.0)], axis=0)
    so = jnp.concatenate([jnp.zeros((n_ctx, HEAD_DIM), F32), jnp.where(even, 0.0, sin)], axis=0)
    return c, se, so


def kernel(x, c, ctx, c_ctx, ada_w, ada_b, norm1_g, w_in, q_norm_g, k_norm_g, na_rel_bias, out_norm_g,
           w_out, norm2_g, router_w, router_bias, exp_w_gate, exp_w_up, exp_w_down, shared_w_gate,
           shared_w_up, shared_w_down, final_g):
    b, n, d = x.shape
    l = ctx.shape[1]
    depth = ada_w.shape[0]
    assert b == 1
    xl = x[0]
    xc = ctx[0]
    mod = _adaln(jnp.stack([c_ctx, c[0]], axis=1), ada_w, ada_b)
    tabs_all = _rope_tables(n, l)
    tabs_lat = tuple(t[l:] for t in tabs_all)
    tabs_ctx = tuple(t[:l] for t in tabs_all)
    w_in_b = w_in.astype(BF16)
    w_out_b = w_out.astype(BF16)
    wgu_b = jnp.concatenate([exp_w_gate, exp_w_up], axis=-1).astype(BF16)
    wd_b = exp_w_down.astype(BF16)
    sgu_b = jnp.concatenate([shared_w_gate, shared_w_up], axis=-1).astype(BF16)
    sd_b = shared_w_down.astype(BF16)

    for layer in range(depth):
        last = layer == depth - 1
        sh1, sc1, g1, sh2, sc2, g2 = [mod[layer, 1, i * d:(i + 1) * d] for i in range(6)]
        csh1, csc1, cg1, csh2, csc2, cg2 = [mod[layer, 0, i * d:(i + 1) * d] for i in range(6)]

        h = _normmod(xl, norm1_g[layer], sc1, sh1)
        hc = _normmod(xc, norm1_g[layer], csc1, csh1)
        p = _matmul(h, w_in_b, layer, BF16, n_out=F_OFF)
        pc = _matmul(hc, w_in_b, layer, BF16, n_out=F_OFF)
        f = _matmul(h, w_in_b, layer, F32, col_block_off=F_OFF // FOURIER_WIDTH, n_out=FOURIER_WIDTH)

        qa_t = _qprep(p, tabs_lat, q_norm_g[layer])
        ka, va_t = _kvprep(pc, p, tabs_all, k_norm_g[layer])
        o_gqa = _gqa_flash(qa_t, ka, va_t)
        o_na = _na(p, pc, na_rel_bias[layer])
        o_four = _fourier(f)
        xl = _outproj(o_gqa, o_na, o_four, out_norm_g[layer], w_out_b, layer, xl, g1)

        if not last:
            fc = _matmul(hc, w_in_b, layer, F32, col_block_off=F_OFF // FOURIER_WIDTH, n_out=FOURIER_WIDTH)
            qa_ct = _qprep(pc, tabs_ctx, q_norm_g[layer])
            oc_gqa = _gqa_flash(qa_ct, ka[:l], va_t[:, :l])
            oc_na = _flash(pc, QN_OFF, pc, KN_OFF, pc, VN_OFF, NA_HEADS, 1)
            oc_four = _fourier_small(fc)
            xc = _outproj(oc_gqa, oc_na, oc_four, out_norm_g[layer], w_out_b, layer, xc, cg1)

        if last:
            h2, h2p = _normmod(xl, norm2_g[layer], sc2, sh2, packed=True)
        else:
            h2, h2p = _normmod_pair(xl, xc, norm2_g[layer], sc2, sh2, csc2, csh2)
        idx, wts, rank, counts = _router(h2, router_w[layer].astype(BF16), router_bias[layer])
        dest, row_tok, block_e, n_used = _moe_tables(idx, rank, counts, MOE_ROWS)
        ys = _experts(h2p, row_tok, block_e, n_used, wgu_b, wd_b, layer)
        ysh = _shared(h2, sgu_b, sd_b, layer)
        xl = _combine(dest[:n], wts[:n], ys, ysh, xl, g2, final_g=final_g if last else None)
        if not last:
            xc = _combine(dest[n:], wts[n:], ys, ysh, xc, cg2, ysh_row_off=n)
    return xl[None]
```

```python
import functools
import math

import numpy as np
import jax
import jax.numpy as jnp
from jax import lax
from jax.experimental import pallas as pl
from jax.experimental.pallas import tpu as pltpu

F32 = jnp.float32
BF16 = jnp.bfloat16
F8 = jnp.float8_e4m3fn
U32 = jnp.uint32
I32 = jnp.int32

HEAD_DIM = 128
GRID_W = 64
GQA_HEADS = 16
GQA_KV_HEADS = 4
NA_HEADS = 8
NA_WIN_ROWS = 8
NA_WIN_COLS = 16
FOURIER_GROUPS = 8
GQA_WIDTH = GQA_HEADS * HEAD_DIM
GQA_KV_WIDTH = GQA_KV_HEADS * HEAD_DIM
NA_WIDTH = NA_HEADS * HEAD_DIM
FOURIER_WIDTH = FOURIER_GROUPS * HEAD_DIM
MIX_WIDTH = GQA_WIDTH + NA_WIDTH + FOURIER_WIDTH
IN_WIDTH = GQA_WIDTH + 2 * GQA_KV_WIDTH + 3 * NA_WIDTH + FOURIER_WIDTH
QA_OFF = 0
KA_OFF = GQA_WIDTH
VA_OFF = KA_OFF + GQA_KV_WIDTH
QN_OFF = VA_OFF + GQA_KV_WIDTH
KN_OFF = QN_OFF + NA_WIDTH
VN_OFF = KN_OFF + NA_WIDTH
F_OFF = VN_OFF + NA_WIDTH
ROPE_THETA = 10000.0
N_EXPERTS = 64
TOP_K = 8
N_EXPERT_GROUPS = 8
TOPK_GROUPS = 4
ROUTED_SCALE = 2.5
EPS = 1e-6
LOG2E = 1.4426950408889634
MASK_VALUE = -1e30

VMEM_LIMIT = 56 * 1024 * 1024
MOE_ROWS = 256
EXPERT_K_CHUNKS = 8
COMBINE_TOKENS = 64


def _tile(m, pref, mult=8):
    for t in range(min(pref, m), 0, -1):
        if m % t == 0 and t % mult == 0:
            return t
    return m


def _params(sem):
    return pltpu.CompilerParams(dimension_semantics=sem, vmem_limit_bytes=VMEM_LIMIT)


def _sigmoid(x):
    return 1.0 / (1.0 + jnp.exp(-x))


def _pack_halves(y):
    w = y.shape[1] // 2
    lo = lax.bitcast_convert_type(y[:, :w].astype(BF16).astype(F32), U32) >> 16
    hi = lax.bitcast_convert_type(y[:, w:].astype(BF16).astype(F32), U32) & jnp.uint32(0xFFFF0000)
    return lo | hi


def _unpack_halves(p):
    lo = lax.bitcast_convert_type(p << 16, F32)
    hi = lax.bitcast_convert_type(p & jnp.uint32(0xFFFF0000), F32)
    return lo, hi


def _adaln_kernel(s_ref, w_ref, b_ref, o_ref):
    d = w_ref.shape[1]
    tn = w_ref.shape[2]
    ch = _tile(d, 256)

    def body(r, acc):
        a0, a1 = acc
        r0 = pl.multiple_of(r * ch, ch)
        w = w_ref[0, pl.ds(r0, ch), :]
        s = s_ref[pl.ds(r0, ch), :]
        s = s * _sigmoid(s)
        p0 = (w * s[:, 0:1]).reshape(ch // 8, 8, tn).sum(axis=0)
        p1 = (w * s[:, 1:2]).reshape(ch // 8, 8, tn).sum(axis=0)
        return a0 + p0, a1 + p1

    z = jnp.zeros((8, tn), F32)
    a0, a1 = lax.fori_loop(0, d // ch, body, (z, z))
    o_ref[0, 0:1, :] = a0.sum(axis=0, keepdims=True) + b_ref[0]
    o_ref[0, 1:2, :] = a1.sum(axis=0, keepdims=True) + b_ref[0]


def _adaln(s_cols, ada_w, ada_b):
    depth, d, n6 = ada_w.shape
    tn = _tile(n6, 512, 128)
    return pl.pallas_call(
        _adaln_kernel,
        grid=(depth, n6 // tn),
        in_specs=[
            pl.BlockSpec((d, 2), lambda l, j: (0, 0)),
            pl.BlockSpec((1, d, tn), lambda l, j: (l, 0, j)),
            pl.BlockSpec((1, 1, tn), lambda l, j: (l, 0, j)),
        ],
        out_specs=pl.BlockSpec((1, 2, tn), lambda l, j: (l, 0, j)),
        out_shape=jax.ShapeDtypeStruct((depth, 2, n6), F32),
        compiler_params=_params(("arbitrary", "arbitrary")),
        name="adaln",
    )(s_cols, ada_w, ada_b.reshape(depth, 1, n6))


def _normmod_kernel(x_ref, g_ref, sc_ref, sh_ref, *o_refs):
    x = x_ref[...]
    ms = jnp.mean(x * x, axis=-1, keepdims=True)
    y = x * lax.rsqrt(ms + EPS) * g_ref[...]
    h = y * (1.0 + sc_ref[...]) + sh_ref[...]
    o_refs[0][...] = h.astype(BF16)
    if len(o_refs) > 1:
        o_refs[1][...] = _pack_halves(h)


def _normmod(x, g, sc, sh, packed=False):
    m, d = x.shape
    tm = _tile(m, 256)
    vec = pl.BlockSpec((1, d), lambda i: (0, 0))
    out_shape = [jax.ShapeDtypeStruct((m, d), BF16)]
    out_specs = [pl.BlockSpec((tm, d), lambda i: (i, 0))]
    if packed:
        out_shape.append(jax.ShapeDtypeStruct((m, d // 2), U32))
        out_specs.append(pl.BlockSpec((tm, d // 2), lambda i: (i, 0)))
    res = pl.pallas_call(
        _normmod_kernel,
        grid=(m // tm,),
        in_specs=[pl.BlockSpec((tm, d), lambda i: (i, 0)), vec, vec, vec],
        out_specs=out_specs,
        out_shape=out_shape,
        compiler_params=_params(("arbitrary",)),
        name="normmod",
    )(x, g.reshape(1, d), sc.reshape(1, d), sh.reshape(1, d))
    return res if packed else res[0]


def _normmod_pair_kernel(xa_ref, xb_ref, g_ref, sca_ref, sha_ref, scb_ref, shb_ref, o_ref, op_ref, *, na_blocks):
    i = pl.program_id(0)

    @pl.when(i < na_blocks)
    def _():
        _normmod_kernel(xa_ref, g_ref, sca_ref, sha_ref, o_ref, op_ref)

    @pl.when(i >= na_blocks)
    def _():
        _normmod_kernel(xb_ref, g_ref, scb_ref, shb_ref, o_ref, op_ref)


def _normmod_pair(xa, xb, g, sca, sha, scb, shb):
    ma, d = xa.shape
    mb = xb.shape[0]
    tm = _tile(math.gcd(ma, mb), 256)
    nab = ma // tm
    vec = pl.BlockSpec((1, d), lambda i: (0, 0))
    return pl.pallas_call(
        functools.partial(_normmod_pair_kernel, na_blocks=nab),
        grid=((ma + mb) // tm,),
        in_specs=[pl.BlockSpec((tm, d), lambda i: (jnp.minimum(i, nab - 1), 0)),
                  pl.BlockSpec((tm, d), lambda i: (jnp.maximum(i - nab, 0), 0)),
                  vec, vec, vec, vec, vec],
        out_specs=[pl.BlockSpec((tm, d), lambda i: (i, 0)), pl.BlockSpec((tm, d // 2), lambda i: (i, 0))],
        out_shape=[jax.ShapeDtypeStruct((ma + mb, d), BF16), jax.ShapeDtypeStruct((ma + mb, d // 2), U32)],
        compiler_params=_params(("arbitrary",)),
        name="normmod_pair",
    )(xa, xb, g.reshape(1, d), sca.reshape(1, d), sha.reshape(1, d), scb.reshape(1, d), shb.reshape(1, d))


def _mm_kernel(a_ref, b_ref, o_ref):
    o_ref[...] = jnp.dot(a_ref[...], b_ref[...], preferred_element_type=F32).astype(o_ref.dtype)


def _matmul(a, b, layer, out_dtype, col_block_off=0, n_out=None, tm_pref=1024, tn_pref=1024):
    m, k = a.shape
    n_out = b.shape[2] if n_out is None else n_out
    tm = _tile(m, tm_pref)
    tn = _tile(n_out, tn_pref, 128)
    return pl.pallas_call(
        _mm_kernel,
        grid=(m // tm, n_out // tn),
        in_specs=[
            pl.BlockSpec((tm, k), lambda i, j: (i, 0)),
            pl.BlockSpec((None, k, tn), lambda i, j: (layer, 0, j + col_block_off)),
        ],
        out_specs=pl.BlockSpec((tm, tn), lambda i, j: (i, j)),
        out_shape=jax.ShapeDtypeStruct((m, n_out), out_dtype),
        compiler_params=_params(("arbitrary", "arbitrary")),
        name="matmul",
    )(a, b)


def _head_norm_rope(x, g, c, se, so):
    ms = jnp.mean(x * x, axis=-1, keepdims=True)
    y = x * lax.rsqrt(ms + EPS) * g
    return y * c + pltpu.roll(y, HEAD_DIM - 1, 1) * se + pltpu.roll(y, 1, 1) * so


def _qprep_kernel(p_ref, c_ref, se_ref, so_ref, g_ref, o_ref, *, nheads, scale):
    c, se, so, g = c_ref[...], se_ref[...], so_ref[...], g_ref[...]
    for h in range(nheads):
        sl = slice(h * HEAD_DIM, (h + 1) * HEAD_DIM)
        r = _head_norm_rope(p_ref[:, sl].astype(F32), g, c, se, so) * scale
        o_ref[sl, :] = r.T.astype(o_ref.dtype)


def _qprep(p, tabs, g):
    m = p.shape[0]
    tq = _tile(m, 512, 128)
    tab = pl.BlockSpec((tq, HEAD_DIM), lambda i: (i, 0))
    return pl.pallas_call(
        functools.partial(_qprep_kernel, nheads=GQA_HEADS, scale=HEAD_DIM ** -0.5 * LOG2E),
        grid=(m // tq,),
        in_specs=[pl.BlockSpec((tq, GQA_WIDTH), lambda i: (i, QA_OFF // GQA_WIDTH)), tab, tab, tab,
                  pl.BlockSpec((1, HEAD_DIM), lambda i: (0, 0))],
        out_specs=pl.BlockSpec((GQA_WIDTH, tq), lambda i: (0, i)),
        out_shape=jax.ShapeDtypeStruct((GQA_WIDTH, m), BF16),
        compiler_params=_params(("arbitrary",)),
        name="qprep",
    )(p, *tabs, g.reshape(1, HEAD_DIM))


VT_ROWS = HEAD_DIM + 16


def _kvprep_kernel(pc_ref, pl_ref, c_ref, se_ref, so_ref, g_ref, k_ref, vt_ref, *, n_ctx_blocks):
    i = pl.program_id(0)
    c, se, so, g = c_ref[...], se_ref[...], so_ref[...], g_ref[...]
    t = k_ref.shape[0]
    ones_row = jnp.where(lax.broadcasted_iota(I32, (VT_ROWS - HEAD_DIM, t), 0) == 0, 1.0, 0.0).astype(BF16)

    def run(p_ref):
        for h in range(GQA_KV_HEADS):
            sl = slice(h * HEAD_DIM, (h + 1) * HEAD_DIM)
            k_ref[:, sl] = _head_norm_rope(p_ref[:, sl].astype(F32), g, c, se, so).astype(k_ref.dtype)
            v = p_ref[:, GQA_KV_WIDTH + h * HEAD_DIM:GQA_KV_WIDTH + (h + 1) * HEAD_DIM].astype(F32)
            vt_ref[h * VT_ROWS:h * VT_ROWS + HEAD_DIM, :] = v.T.astype(vt_ref.dtype)
            vt_ref[h * VT_ROWS + HEAD_DIM:(h + 1) * VT_ROWS, :] = ones_row

    @pl.when(i < n_ctx_blocks)
    def _():
        run(pc_ref)

    @pl.when(i >= n_ctx_blocks)
    def _():
        run(pl_ref)


def _kvprep(p_ctx, p_lat, tabs, g):
    l, n = p_ctx.shape[0], p_lat.shape[0]
    t = _tile(math.gcd(l, n), 256, 128)
    ncb = l // t
    w = 2 * GQA_KV_WIDTH
    cb = KA_OFF // w
    tab = pl.BlockSpec((t, HEAD_DIM), lambda i: (i, 0))
    return pl.pallas_call(
        functools.partial(_kvprep_kernel, n_ctx_blocks=ncb),
        grid=((l + n) // t,),
        in_specs=[pl.BlockSpec((t, w), lambda i: (jnp.minimum(i, ncb - 1), cb)),
                  pl.BlockSpec((t, w), lambda i: (jnp.maximum(i - ncb, 0), cb)),
                  tab, tab, tab, pl.BlockSpec((1, HEAD_DIM), lambda i: (0, 0))],
        out_specs=[pl.BlockSpec((t, GQA_KV_WIDTH), lambda i: (i, 0)),
                   pl.BlockSpec((GQA_KV_HEADS * VT_ROWS, t), lambda i: (0, i))],
        out_shape=[jax.ShapeDtypeStruct((l + n, GQA_KV_WIDTH), BF16),
                   jax.ShapeDtypeStruct((GQA_KV_HEADS * VT_ROWS, l + n), BF16)],
        compiler_params=_params(("arbitrary",)),
        name="kvprep",
    )(p_ctx, p_lat, *tabs, g.reshape(1, HEAD_DIM))


def _gqa_flash_kernel(qT_ref, k_ref, vT_ref, o_ref, m_ref, acc_ref, *, group, cb):
    j = pl.program_id(2)
    tq = qT_ref.shape[1]

    @pl.when(j == 0)
    def _():
        m_ref[...] = jnp.full(m_ref.shape, -jnp.inf, F32)
        acc_ref[...] = jnp.zeros(acc_ref.shape, F32)

    k = k_ref[...].astype(F8)
    vT = vT_ref[...].astype(F8)
    nsub = tq // cb
    nblk = group * nsub
    st = {}

    def scores(c):
        h, sub = divmod(c, nsub)
        cols = slice(c * cb, (c + 1) * cb)
        qT = qT_ref[h * HEAD_DIM:(h + 1) * HEAD_DIM, sub * cb:(sub + 1) * cb]
        sT = jnp.dot(k, qT.astype(F8), preferred_element_type=F32)
        m_prev = m_ref[:, cols]
        m_new = jnp.maximum(m_prev, jnp.max(sT, axis=0, keepdims=True))
        m_ref[:, cols] = m_new
        st[c] = (sT, m_new, jnp.exp2(m_prev - m_new))

    def probs(c):
        sT, m_new, alpha = st[c]
        st[c] = (jnp.exp2(sT - m_new).astype(F8), alpha)

    def accumulate(c):
        cols = slice(c * cb, (c + 1) * cb)
        pT, alpha = st.pop(c)
        acc_ref[:, cols] = alpha * acc_ref[:, cols] + jnp.dot(vT, pT, preferred_element_type=F32)

    scores(0)
    for c in range(nblk):
        probs(c)
        if c + 1 < nblk:
            scores(c + 1)
        accumulate(c)

    @pl.when(j == pl.num_programs(2) - 1)
    def _():
        for h in range(group):
            cols = slice(h * tq, (h + 1) * tq)
            o = acc_ref[:HEAD_DIM, cols] * (1.0 / acc_ref[HEAD_DIM:HEAD_DIM + 1, cols])
            o_ref[:, h * HEAD_DIM:(h + 1) * HEAD_DIM] = o.T.astype(o_ref.dtype)


def _gqa_flash(qT, k, vT, tq_pref=1024, tk_pref=3328, cb=512):
    nq, tk_total = qT.shape[1], k.shape[0]
    group = GQA_HEADS // GQA_KV_HEADS
    tq = _tile(nq, tq_pref, 128)
    tk = _tile(tk_total, tk_pref, 128)
    gw = group * HEAD_DIM
    return pl.pallas_call(
        functools.partial(_gqa_flash_kernel, group=group, cb=min(cb, tq)),
        grid=(GQA_KV_HEADS, nq // tq, tk_total // tk),
        in_specs=[
            pl.BlockSpec((gw, tq), lambda g, i, j: (g, i)),
            pl.BlockSpec((tk, HEAD_DIM), lambda g, i, j: (j, g)),
            pl.BlockSpec((VT_ROWS, tk), lambda g, i, j: (g, j)),
        ],
        out_specs=pl.BlockSpec((tq, gw), lambda g, i, j: (i, g)),
        out_shape=jax.ShapeDtypeStruct((nq, GQA_WIDTH), BF16),
        scratch_shapes=[
            pltpu.VMEM((1, group * tq), F32),
            pltpu.VMEM((VT_ROWS, group * tq), F32),
        ],
        compiler_params=_params(("arbitrary", "arbitrary", "arbitrary")),
        name="gqa_flash",
    )(qT, k, vT)


def _flash_kernel(q_ref, k_ref, v_ref, o_ref, qs_ref, m_ref, l_ref, acc_ref, *, group, scale):
    j = pl.program_id(2)
    tq = q_ref.shape[0]

    @pl.when(j == 0)
    def _():
        for h in range(group):
            sl = slice(h * HEAD_DIM, (h + 1) * HEAD_DIM)
            qs_ref[h * tq:(h + 1) * tq, :] = (q_ref[:, sl].astype(F32) * scale).astype(BF16)
        m_ref[...] = jnp.full(m_ref.shape, -jnp.inf, F32)
        l_ref[...] = jnp.zeros(l_ref.shape, F32)
        acc_ref[...] = jnp.zeros(acc_ref.shape, F32)

    s = lax.dot_general(qs_ref[...], k_ref[...], (((1,), (1,)), ((), ())), preferred_element_type=F32)
    m_prev = m_ref[...]
    m_new = jnp.maximum(m_prev, jnp.max(s, axis=-1, keepdims=True))
    alpha = jnp.exp2(m_prev - m_new)
    p = jnp.exp2(s - m_new)
    l_ref[...] = alpha * l_ref[...] + jnp.sum(p, axis=-1, keepdims=True)
    acc_ref[...] = alpha * acc_ref[...] + jnp.dot(p.astype(BF16), v_ref[...], preferred_element_type=F32)
    m_ref[...] = m_new

    @pl.when(j == pl.num_programs(2) - 1)
    def _():
        inv = 1.0 / l_ref[...]
        for h in range(group):
            sl = slice(h * HEAD_DIM, (h + 1) * HEAD_DIM)
            o_ref[:, sl] = (acc_ref[h * tq:(h + 1) * tq, :] * inv[h * tq:(h + 1) * tq]).astype(o_ref.dtype)


def _flash(q, q_off, k, k_off, v, v_off, n_kv_heads, group, tq_pref=512, tk_pref=1280):
    nq, tk_total = q.shape[0], k.shape[0]
    tq = _tile(nq, tq_pref)
    tk = _tile(tk_total, tk_pref, 128)
    gw = group * HEAD_DIM
    scale = HEAD_DIM ** -0.5 * LOG2E
    return pl.pallas_call(
        functools.partial(_flash_kernel, group=group, scale=scale),
        grid=(n_kv_heads, nq // tq, tk_total // tk),
        in_specs=[
            pl.BlockSpec((tq, gw), lambda g, i, j: (i, q_off // gw + g)),
            pl.BlockSpec((tk, HEAD_DIM), lambda g, i, j: (j, k_off // HEAD_DIM + g)),
            pl.BlockSpec((tk, HEAD_DIM), lambda g, i, j: (j, v_off // HEAD_DIM + g)),
        ],
        out_specs=pl.BlockSpec((tq, gw), lambda g, i, j: (i, g)),
        out_shape=jax.ShapeDtypeStruct((nq, n_kv_heads * gw), BF16),
        scratch_shapes=[
            pltpu.VMEM((group * tq, HEAD_DIM), BF16),
            pltpu.VMEM((group * tq, 1), F32),
            pltpu.VMEM((group * tq, 1), F32),
            pltpu.VMEM((group * tq, HEAD_DIM), F32),
        ],
        compiler_params=_params(("arbitrary", "arbitrary", "arbitrary")),
        name="flash",
    )(q, k, v)


NA_QROWS = 8
NA_BAND = 16


def _na_kernel(q_ref, k0, k1, k2, k3, v0, v1, v2, v3, kc_ref, vc_ref, tb_ref, o_ref, kb_ref, vb_ref,
               *, rows):
    b = pl.program_id(0)
    quarter = NA_BAND * GRID_W // 4
    for d, (kr, vr) in enumerate(((k0, v0), (k1, v1), (k2, v2), (k3, v3))):
        kb_ref[d * quarter:(d + 1) * quarter, :] = kr[...]
        vb_ref[d * quarter:(d + 1) * quarter, :] = vr[...]
    band_start = jnp.clip(NA_QROWS * b - NA_WIN_ROWS // 2, 0, rows - NA_BAND)
    scale = HEAD_DIM ** -0.5
    win = NA_WIN_ROWS * GRID_W

    def body(a, carry):
        r = NA_QROWS * b + a
        r_start = jnp.clip(r - NA_WIN_ROWS // 2, 0, rows - NA_WIN_ROWS)
        dd = r - r_start
        koff = pl.multiple_of((r_start - band_start) * GRID_W, GRID_W)
        qoff = pl.multiple_of(a * GRID_W, GRID_W)
        q_a = q_ref[pl.ds(qoff, GRID_W), :]
        kw = kb_ref[pl.ds(koff, win), :]
        vw = vb_ref[pl.ds(koff, win), :]
        for h in range(NA_HEADS):
            sl = slice(h * HEAD_DIM, (h + 1) * HEAD_DIM)
            qh = q_a[:, sl]
            dn = (((1,), (1,)), ((), ()))
            s_nb = lax.dot_general(qh, kw[:, sl], dn, preferred_element_type=F32) * scale + tb_ref[h, dd]
            s_c = lax.dot_general(qh, kc_ref[:, sl], dn, preferred_element_type=F32) * scale
            m = jnp.maximum(jnp.max(s_nb, axis=-1, keepdims=True), jnp.max(s_c, axis=-1, keepdims=True))
            p_nb = jnp.exp(s_nb - m)
            p_c = jnp.exp(s_c - m)
            l = jnp.sum(p_nb, axis=-1, keepdims=True) + jnp.sum(p_c, axis=-1, keepdims=True)
            o = (jnp.dot(p_c.astype(BF16), vc_ref[:, sl], preferred_element_type=F32)
                 + jnp.dot(p_nb.astype(BF16), vw[:, sl], preferred_element_type=F32))
            o_ref[pl.ds(qoff, GRID_W), sl] = (o / l).astype(o_ref.dtype)
        return carry

    lax.fori_loop(0, NA_QROWS, body, 0)


def _na_bias_table(rel_bias):
    h = rel_bias.shape[0]
    wr, wc = NA_WIN_ROWS, NA_WIN_COLS
    nro, nco = 2 * wr - 1, 2 * wc - 1
    p = 2 * GRID_W
    w = jnp.full((h, nro, p), MASK_VALUE, F32).at[:, :, :nco].set(rel_bias.astype(F32))
    sk = jnp.broadcast_to(w[:, :, None, :], (h, nro, GRID_W, p)).reshape(h, nro, GRID_W * p)
    sk = sk[:, :, :GRID_W * (p - 1)].reshape(h, nro, GRID_W, p - 1)[:, :, :, wc - 1:wc - 1 + GRID_W]
    qc = np.arange(GRID_W)
    kc = np.arange(GRID_W)
    col_start = np.clip(qc - wc // 2, 0, GRID_W - wc)
    valid = (kc[None, :] >= col_start[:, None]) & (kc[None, :] < col_start[:, None] + wc)
    sk = jnp.where(jnp.asarray(valid)[None, None], sk, MASK_VALUE)
    tb = jnp.stack([sk[:, wr - 1 - dd:2 * wr - 1 - dd] for dd in range(wr)], axis=1)
    return tb.transpose(0, 1, 3, 2, 4).reshape(h, wr, GRID_W, wr * GRID_W)


def _na(p_lat, p_ctx, rel_bias):
    n, l = p_lat.shape[0], p_ctx.shape[0]
    rows = n // GRID_W
    assert rows % NA_QROWS == 0 and rows >= NA_BAND
    tq = NA_QROWS * GRID_W
    quarter = NA_BAND * GRID_W // 4
    nquart = n // quarter

    def band_block(d, col):
        return pl.BlockSpec((quarter, NA_WIDTH),
                            lambda b: (jnp.clip(2 * b - 1, 0, nquart - 4) + d, col))

    kcol, vcol = KN_OFF // NA_WIDTH, VN_OFF // NA_WIDTH
    tb = _na_bias_table(rel_bias)
    return pl.pallas_call(
        functools.partial(_na_kernel, rows=rows),
        grid=(rows // NA_QROWS,),
        in_specs=[pl.BlockSpec((tq, NA_WIDTH), lambda b: (b, QN_OFF // NA_WIDTH))]
        + [band_block(d, kcol) for d in range(4)]
        + [band_block(d, vcol) for d in range(4)]
        + [pl.BlockSpec((l, NA_WIDTH), lambda b: (0, kcol)),
           pl.BlockSpec((l, NA_WIDTH), lambda b: (0, vcol)),
           pl.BlockSpec(tb.shape, lambda b: (0, 0, 0, 0))],
        out_specs=pl.BlockSpec((tq, NA_WIDTH), lambda b: (b, 0)),
        out_shape=jax.ShapeDtypeStruct((n, NA_WIDTH), BF16),
        scratch_shapes=[pltpu.VMEM((NA_BAND * GRID_W, NA_WIDTH), BF16),
                        pltpu.VMEM((NA_BAND * GRID_W, NA_WIDTH), BF16)],
        compiler_params=_params(("arbitrary",)),
        name="natten",
    )(p_lat, *([p_lat] * 8), p_ctx, p_ctx, tb)


def _dft_cs(n):
    k = np.arange(n, dtype=np.float64)
    ang = 2.0 * np.pi * np.outer(k, k) / n
    return np.cos(ang), np.sin(ang)


def _f1_kernel(x_ref, da_ref, db_ref, twr_ref, twi_ref, u_ref, *, tb, n_hi):
    w = FOURIER_WIDTH
    for j in range(tb):
        x = x_ref[:, j, :].astype(BF16)
        zs = [jnp.dot(x[:, g * HEAD_DIM:(g + 1) * HEAD_DIM], da_ref[...], preferred_element_type=F32)
              for g in range(FOURIER_GROUPS)]
        zr = jnp.concatenate([z[:, :HEAD_DIM] for z in zs], axis=1)
        zi = jnp.concatenate([z[:, HEAD_DIM:] for z in zs], axis=1)
        zst = jnp.concatenate([zr, zi], axis=0).astype(BF16)
        u = jnp.dot(db_ref[...], zst, preferred_element_type=F32)
        ur, ui = u[:n_hi], u[n_hi:]
        tr, ti = twr_ref[j], twi_ref[j]
        u_ref[:, j, :w] = ur * tr - ui * ti
        u_ref[:, j, w:] = ur * ti + ui * tr


def _f2_kernel(u_ref, dc_ref, y_ref, *, ta):
    w = FOURIER_WIDTH
    for j in range(ta):
        u = u_ref[j]
        ust = jnp.concatenate([u[:, :w], u[:, w:]], axis=0).astype(BF16)
        y_ref[:, j, :] = jnp.dot(dc_ref[...], ust, preferred_element_type=F32)


def _fourier(xf):
    n, w = xf.shape
    n_lo = 128
    n_hi = n // n_lo
    assert n_hi * n_lo == n and n_hi % 8 == 0
    cc, sc = _dft_cs(HEAD_DIM)
    da = jnp.asarray(np.concatenate([cc, -sc], axis=1), BF16)
    cn, sn = _dft_cs(n_hi)
    db = jnp.asarray(np.block([[cn, sn], [-sn, cn]]), BF16)
    ang = 2.0 * np.pi * np.outer(np.arange(n_lo), np.arange(n_hi)) / n
    twr = jnp.asarray(np.cos(ang)[:, :, None], F32)
    twi = jnp.asarray(-np.sin(ang)[:, :, None], F32)
    cl, sl = _dft_cs(n_lo)
    dc = jnp.asarray(np.concatenate([cl, sl], axis=1), BF16)
    tb = 8
    u = pl.pallas_call(
        functools.partial(_f1_kernel, tb=tb, n_hi=n_hi),
        grid=(n_lo // tb,),
        in_specs=[pl.BlockSpec((n_hi, tb, w), lambda j: (0, j, 0)),
                  pl.BlockSpec(da.shape, lambda j: (0, 0)),
                  pl.BlockSpec(db.shape, lambda j: (0, 0)),
                  pl.BlockSpec((tb, n_hi, 1), lambda j: (j, 0, 0)),
                  pl.BlockSpec((tb, n_hi, 1), lambda j: (j, 0, 0))],
        out_specs=pl.BlockSpec((n_hi, tb, 2 * w), lambda j: (0, j, 0)),
        out_shape=jax.ShapeDtypeStruct((n_hi, n_lo, 2 * w), F32),
        compiler_params=_params(("arbitrary",)),
        name="fourier_stage1",
    )(xf.reshape(n_hi, n_lo, w), da, db, twr, twi)
    ta = 8
    y = pl.pallas_call(
        functools.partial(_f2_kernel, ta=ta),
        grid=(n_hi // ta,),
        in_specs=[pl.BlockSpec((ta, n_lo, 2 * w), lambda i: (i, 0, 0)),
                  pl.BlockSpec(dc.shape, lambda i: (0, 0))],
        out_specs=pl.BlockSpec((n_lo, ta, w), lambda i: (0, i, 0)),
        out_shape=jax.ShapeDtypeStruct((n_lo, n_hi, w), F32),
        compiler_params=_params(("arbitrary",)),
        name="fourier_stage2",
    )(u, dc)
    return y.reshape(n, w)


def _fourier_small_kernel(x_ref, dch_ref, dpos_ref, y_ref):
    x = x_ref[...].astype(BF16)
    for g in range(FOURIER_GROUPS):
        sl = slice(g * HEAD_DIM, (g + 1) * HEAD_DIM)
        a = jnp.dot(x[:, sl], dch_ref[...], preferred_element_type=F32)
        ast = jnp.concatenate([a[:, :HEAD_DIM], a[:, HEAD_DIM:]], axis=0).astype(BF16)
        y_ref[:, sl] = jnp.dot(dpos_ref[...], ast, preferred_element_type=F32)


def _fourier_small(xf):
    t, w = xf.shape
    cc, sc = _dft_cs(HEAD_DIM)
    dch = jnp.asarray(np.concatenate([cc, sc], axis=1), BF16)
    cp, sp = _dft_cs(t)
    dpos = jnp.asarray(np.concatenate([cp, -sp], axis=1), BF16)
    return pl.pallas_call(
        _fourier_small_kernel,
        grid=(1,),
        in_specs=[pl.BlockSpec((t, w), lambda i: (0, 0)),
                  pl.BlockSpec(dch.shape, lambda i: (0, 0)),
                  pl.BlockSpec(dpos.shape, lambda i: (0, 0))],
        out_specs=pl.BlockSpec((t, w), lambda i: (0, 0)),
        out_shape=jax.ShapeDtypeStruct((t, w), F32),
        compiler_params=_params(("arbitrary",)),
        name="fourier_small",
    )(xf, dch, dpos)


def _outproj_kernel(og_ref, on_ref, of_ref, g_ref, w_ref, x_ref, gate_ref, o_ref, m_ref):
    @pl.when(pl.program_id(1) == 0)
    def _():
        def nrm(v, g):
            vf = v.astype(F32)
            return (vf * lax.rsqrt(jnp.mean(vf * vf, axis=-1, keepdims=True) + EPS) * g).astype(BF16)

        a, b = GQA_WIDTH, GQA_WIDTH + NA_WIDTH
        m_ref[:, :a] = nrm(og_ref[...], g_ref[:, :a])
        m_ref[:, a:b] = nrm(on_ref[...], g_ref[:, a:b])
        m_ref[:, b:] = nrm(of_ref[...], g_ref[:, b:])

    acc = jnp.dot(m_ref[...], w_ref[...], preferred_element_type=F32)
    o_ref[...] = x_ref[...] + gate_ref[...] * acc


def _outproj(o_gqa, o_na, o_four, g_out, w_out, layer, x, gate):
    m, d = x.shape
    tm = _tile(m, 512)
    tn = _tile(d, 1024, 128)
    return pl.pallas_call(
        _outproj_kernel,
        grid=(m // tm, d // tn),
        in_specs=[pl.BlockSpec((tm, GQA_WIDTH), lambda i, j: (i, 0)),
                  pl.BlockSpec((tm, NA_WIDTH), lambda i, j: (i, 0)),
                  pl.BlockSpec((tm, FOURIER_WIDTH), lambda i, j: (i, 0)),
                  pl.BlockSpec((1, MIX_WIDTH), lambda i, j: (0, 0)),
                  pl.BlockSpec((None, MIX_WIDTH, tn), lambda i, j: (layer, 0, j)),
                  pl.BlockSpec((tm, tn), lambda i, j: (i, j)),
                  pl.BlockSpec((1, tn), lambda i, j: (0, j))],
        out_specs=pl.BlockSpec((tm, tn), lambda i, j: (i, j)),
        out_shape=jax.ShapeDtypeStruct((m, d), F32),
        scratch_shapes=[pltpu.VMEM((tm, MIX_WIDTH), BF16)],
        compiler_params=_params(("arbitrary", "arbitrary")),
        name="outproj",
    )(o_gqa, o_na, o_four, g_out.reshape(1, MIX_WIDTH), w_out, x, gate.reshape(1, d))


def _router_kernel(h_ref, rw_ref, rb_ref, idx_ref, wt_ref, rank_ref, cnt_ref, carry_ref):
    step = pl.program_id(0)

    @pl.when(step == 0)
    def _():
        carry_ref[...] = jnp.zeros(carry_ref.shape, F32)

    tm = h_ref.shape[0]
    e = N_EXPERTS
    gsz = e // N_EXPERT_GROUPS
    neg = -jnp.inf
    logits = jnp.dot(h_ref[...], rw_ref[...], preferred_element_type=F32)
    scores = _sigmoid(logits)
    biased = scores + rb_ref[...]
    lane = lax.broadcasted_iota(I32, (tm, e), 1).astype(F32)
    lane_grp = lax.broadcasted_iota(I32, (tm, e), 1) // gsz

    def first_argmax(v):
        m = jnp.max(v, axis=-1, keepdims=True)
        return m, jnp.min(jnp.where(v == m, lane, float(e)), axis=-1, keepdims=True)

    gs = []
    for g in range(N_EXPERT_GROUPS):
        vg = jnp.where(lane_grp == g, biased, neg)
        m1, i1 = first_argmax(vg)
        m2 = jnp.max(jnp.where(lane == i1, neg, vg), axis=-1, keepdims=True)
        gs.append(m1 + m2)
    masked = jnp.full((tm, e), neg, F32)
    for g in range(N_EXPERT_GROUPS):
        ahead = jnp.zeros((tm, 1), F32)
        for g2 in range(N_EXPERT_GROUPS):
            if g2 == g:
                continue
            better = (gs[g2] > gs[g]) | ((gs[g2] == gs[g]) & (g2 < g))
            ahead = ahead + jnp.where(better, 1.0, 0.0)
        masked = jnp.where((lane_grp == g) & (ahead < TOPK_GROUPS), biased, masked)
    sel = jnp.zeros((tm, e), F32)
    idxs, ws = [], []
    cur = masked
    for _ in range(TOP_K):
        _, ik = first_argmax(cur)
        hit = lane == ik
        ws.append(jnp.sum(jnp.where(hit, scores, 0.0), axis=-1, keepdims=True))
        cur = jnp.where(hit, neg, cur)
        sel = sel + jnp.where(hit, 1.0, 0.0)
        idxs.append(ik)
    wsum = ws[0]
    for wk in ws[1:]:
        wsum = wsum + wk
    rr = lax.broadcasted_iota(I32, (tm, tm), 0)
    cc = lax.broadcasted_iota(I32, (tm, tm), 1)
    tri = jnp.where(cc < rr, 1.0, 0.0).astype(BF16)
    pref = jnp.dot(tri, sel.astype(BF16), preferred_element_type=F32) + carry_ref[...]
    ranks = [jnp.sum(jnp.where(lane == ik, pref, 0.0), axis=-1, keepdims=True) for ik in idxs]
    carry_ref[...] = carry_ref[...] + jnp.sum(sel, axis=0, keepdims=True)

    lane_o = lax.broadcasted_iota(I32, (tm, 128), 1)
    o_idx = jnp.zeros((tm, 128), F32)
    o_w = jnp.zeros((tm, 128), F32)
    o_rank = jnp.zeros((tm, 128), F32)
    for k in range(TOP_K):
        o_idx = jnp.where(lane_o == k, idxs[k], o_idx)
        o_w = jnp.where(lane_o == k, ws[k] / wsum * ROUTED_SCALE, o_w)
        o_rank = jnp.where(lane_o == k, ranks[k], o_rank)
    idx_ref[...] = o_idx.astype(I32)
    wt_ref[...] = o_w
    rank_ref[...] = o_rank.astype(I32)
    cnt_ref[...] = jnp.broadcast_to(carry_ref[...], cnt_ref.shape)


def _router(h, rw, rb):
    t, d = h.shape
    tm = _tile(t, 256)
    blk = pl.BlockSpec((tm, 128), lambda i: (i, 0))
    idx, wt, rank, cnt = pl.pallas_call(
        _router_kernel,
        grid=(t // tm,),
        in_specs=[pl.BlockSpec((tm, d), lambda i: (i, 0)),
                  pl.BlockSpec((d, N_EXPERTS), lambda i: (0, 0)),
                  pl.BlockSpec((1, N_EXPERTS), lambda i: (0, 0))],
        out_specs=[blk, blk, blk, pl.BlockSpec((8, N_EXPERTS), lambda i: (0, 0))],
        out_shape=[jax.ShapeDtypeStruct((t, 128), I32), jax.ShapeDtypeStruct((t, 128), F32),
                   jax.ShapeDtypeStruct((t, 128), I32), jax.ShapeDtypeStruct((8, N_EXPERTS), F32)],
        scratch_shapes=[pltpu.VMEM((1, N_EXPERTS), F32)],
        compiler_params=_params(("arbitrary",)),
        name="router",
    )(h, rw, rb.reshape(1, N_EXPERTS))
    return idx[:, :TOP_K], wt[:, :TOP_K], rank[:, :TOP_K], cnt[0].astype(I32)


def _expert_kernel(be_ref, nu_ref, rt_hbm, hp_hbm, wgu_ref, wd_ref, o_ref, idx_smem, xbuf, idx_sem, x_sem):
    b = pl.program_id(0)
    n_used = nu_ref[0]
    n_blocks = pl.num_programs(0)
    bm = xbuf.shape[1]
    dh = xbuf.shape[2]
    ff = wd_ref.shape[1]
    slot = b % 2
    nslot = 1 - slot
    n_chunks = max(1, min(EXPERT_K_CHUNKS, dh // 128))
    kc = dh // n_chunks
    issue_chunks = max(1, n_chunks // 2)
    per = bm // issue_chunks

    def idx_copy(s, sl):
        return pltpu.make_async_copy(rt_hbm.at[s], idx_smem.at[sl], idx_sem.at[sl])

    def row_copy(sl, r, tok):
        return pltpu.make_async_copy(hp_hbm.at[pl.ds(tok, 1), :], xbuf.at[sl, pl.ds(r, 1), :], x_sem.at[sl])

    def slot_wait(sl):
        pltpu.make_async_copy(xbuf.at[sl], xbuf.at[sl], x_sem.at[sl]).wait()

    @pl.when(b == 0)
    def _():
        idx_copy(0, 0).start()
        idx_copy(0, 0).wait()

        def body(r, carry):
            row_copy(0, r, idx_smem[0, r]).start()
            return carry

        lax.fori_loop(0, bm, body, 0)
        idx_copy(1, 1).start()

    @pl.when(b < n_used)
    def _():
        idx_copy(b + 1, nslot).wait()
        slot_wait(slot)
        h = None
        for c in range(n_chunks):
            lo, hi = _unpack_halves(xbuf[slot, :, c * kc:(c + 1) * kc])
            part = (jnp.dot(lo.astype(BF16), wgu_ref[0, c * kc:(c + 1) * kc, :], preferred_element_type=F32)
                    + jnp.dot(hi.astype(BF16), wgu_ref[0, dh + c * kc:dh + (c + 1) * kc, :],
                              preferred_element_type=F32))
            h = part if h is None else h + part
            if c < issue_chunks:
                for r in range(c * per, (c + 1) * per):
                    row_copy(nslot, r, idx_smem[nslot, r]).start()
        idx_copy(jnp.minimum(b + 2, n_blocks - 1), slot).start()
        gate, up = h[:, :ff], h[:, ff:]
        act = (gate * _sigmoid(gate) * up).astype(BF16)
        y = jnp.dot(act, wd_ref[0], preferred_element_type=F32)
        o_ref[...] = _pack_halves(y)

    @pl.when(b >= n_used)
    def _():
        o_ref[...] = jnp.zeros(o_ref.shape, o_ref.dtype)

    @pl.when(b == n_used)
    def _():
        idx_copy(0, nslot).wait()
        slot_wait(slot)


def _experts(hp, row_tok, block_e, n_used, wgu, wd, layer):
    n_blocks, bm = row_tok.shape
    dh = hp.shape[1]
    d = 2 * dh
    ff = wd.shape[2]
    grid_spec = pltpu.PrefetchScalarGridSpec(
        num_scalar_prefetch=2,
        grid=(n_blocks,),
        in_specs=[pl.BlockSpec(memory_space=pl.ANY),
                  pl.BlockSpec(memory_space=pl.ANY),
                  pl.BlockSpec((None, 1, d, 2 * ff), lambda b, be, nu: (layer, be[b], 0, 0)),
                  pl.BlockSpec((None, 1, ff, d), lambda b, be, nu: (layer, be[b], 0, 0))],
        out_specs=pl.BlockSpec((bm, dh), lambda b, be, nu: (b, 0)),
        scratch_shapes=[pltpu.SMEM((2, bm), I32),
                        pltpu.VMEM((2, bm, dh), U32),
                        pltpu.SemaphoreType.DMA((2,)),
                        pltpu.SemaphoreType.DMA((2,))],
    )
    return pl.pallas_call(
        _expert_kernel,
        grid_spec=grid_spec,
        out_shape=jax.ShapeDtypeStruct((n_blocks * bm, dh), U32),
        compiler_params=_params(("arbitrary",)),
        name="experts",
    )(block_e, n_used, row_tok, hp, wgu, wd)


def _shared_kernel(h_ref, wgu_ref, wd_ref, o_ref):
    ff = wd_ref.shape[0]
    h = jnp.dot(h_ref[...], wgu_ref[...], preferred_element_type=F32)
    gate, up = h[:, :ff], h[:, ff:]
    act = (gate * _sigmoid(gate) * up).astype(BF16)
    o_ref[...] = jnp.dot(act, wd_ref[...], preferred_element_type=F32).astype(o_ref.dtype)


def _shared(h, wgu, wd, layer):
    t, d = h.shape
    ff = wd.shape[1]
    tm = _tile(t, 512)
    return pl.pallas_call(
        _shared_kernel,
        grid=(t // tm,),
        in_specs=[pl.BlockSpec((tm, d), lambda i: (i, 0)),
                  pl.BlockSpec((None, d, 2 * ff), lambda i: (layer, 0, 0)),
                  pl.BlockSpec((None, ff, d), lambda i: (layer, 0, 0))],
        out_specs=pl.BlockSpec((tm, d), lambda i: (i, 0)),
        out_shape=jax.ShapeDtypeStruct((t, d), BF16),
        compiler_params=_params(("arbitrary",)),
        name="shared_expert",
    )(h, wgu, wd)


def _combine_kernel(dest_hbm, ys_hbm, w_ref, ysh_ref, x_ref, gate_ref, *rest, final):
    if final:
        fg_ref, o_ref, idx_smem, gbuf, idx_sem, g_sem = rest
    else:
        o_ref, idx_smem, gbuf, idx_sem, g_sem = rest
    s = pl.program_id(0)
    last = pl.num_programs(0) - 1
    tt = gbuf.shape[2]
    dh = gbuf.shape[3]
    slot = s % 2
    nslot = 1 - slot
    per = tt // TOP_K

    def idx_copy(st, sl):
        return pltpu.make_async_copy(dest_hbm.at[st], idx_smem.at[sl], idx_sem.at[sl])

    def row_copy(sl, k, t, row):
        return pltpu.make_async_copy(ys_hbm.at[pl.ds(row, 1), :], gbuf.at[sl, k, pl.ds(t, 1), :], g_sem.at[sl])

    def slot_wait(sl):
        pltpu.make_async_copy(gbuf.at[sl], gbuf.at[sl], g_sem.at[sl]).wait()

    @pl.when(s == 0)
    def _():
        idx_copy(0, 0).start()
        idx_copy(0, 0).wait()

        def body(t, carry):
            for k in range(TOP_K):
                row_copy(0, k, t, idx_smem[0, t * TOP_K + k]).start()
            return carry

        lax.fori_loop(0, tt, body, 0)
        idx_copy(jnp.minimum(1, last), 1).start()

    idx_copy(0, nslot).wait()
    slot_wait(slot)
    w = w_ref[...]
    acc_lo = jnp.zeros((tt, dh), F32)
    acc_hi = jnp.zeros((tt, dh), F32)
    for k in range(TOP_K):
        lo, hi = _unpack_halves(gbuf[slot, k])
        wk = w[:, k:k + 1]
        acc_lo = acc_lo + wk * lo
        acc_hi = acc_hi + wk * hi
        for t in range(k * per, (k + 1) * per):
            for kk in range(TOP_K):
                row_copy(nslot, kk, t, idx_smem[nslot, t * TOP_K + kk]).start()
    idx_copy(jnp.minimum(s + 2, last), slot).start()
    y = jnp.concatenate([acc_lo, acc_hi], axis=1) + ysh_ref[...].astype(F32)
    xn = x_ref[...] + gate_ref[...] * y
    if final:
        xn = xn * lax.rsqrt(jnp.mean(xn * xn, axis=-1, keepdims=True) + EPS) * fg_ref[...]
    o_ref[...] = xn

    @pl.when(s == last)
    def _():
        idx_copy(0, slot).wait()
        slot_wait(nslot)


def _combine(dest, w, ys, ysh, x, gate, final_g=None, ysh_row_off=0):
    t, d = x.shape
    tt = _tile(t, COMBINE_TOKENS)
    dh = d // 2
    final = final_g is not None
    off_blocks, rem = divmod(ysh_row_off, tt)
    assert rem == 0
    in_specs = [pl.BlockSpec(memory_space=pl.ANY),
                pl.BlockSpec(memory_space=pl.ANY),
                pl.BlockSpec((tt, TOP_K), lambda i: (i, 0)),
                pl.BlockSpec((tt, d), lambda i: (i + off_blocks, 0)),
                pl.BlockSpec((tt, d), lambda i: (i, 0)),
                pl.BlockSpec((1, d), lambda i: (0, 0))]
    args = [dest.reshape(t // tt, tt * TOP_K), ys, w, ysh, x, gate.reshape(1, d)]
    if final:
        in_specs.append(pl.BlockSpec((1, d), lambda i: (0, 0)))
        args.append(final_g.reshape(1, d))
    return pl.pallas_call(
        functools.partial(_combine_kernel, final=final),
        grid=(t // tt,),
        in_specs=in_specs,
        out_specs=pl.BlockSpec((tt, d), lambda i: (i, 0)),
        out_shape=jax.ShapeDtypeStruct((t, d), F32),
        scratch_shapes=[pltpu.SMEM((2, tt * TOP_K), I32),
                        pltpu.VMEM((2, TOP_K, tt, dh), U32),
                        pltpu.SemaphoreType.DMA((2,)),
                        pltpu.SemaphoreType.DMA((2,))],
        compiler_params=_params(("arbitrary",)),
        name="combine",
    )(*args)


def _moe_tables(idx, rank, counts, bm):
    t = idx.shape[0]
    e = N_EXPERTS
    tk = t * TOP_K
    padded = (counts + bm - 1) // bm * bm
    pad_end = jnp.cumsum(padded)
    pad_start = pad_end - padded
    onehot = idx[:, :, None] == jnp.arange(e, dtype=I32)[None, None, :]
    dest = jnp.sum(jnp.where(onehot, pad_start[None, None, :], 0), axis=-1) + rank
    n_blocks = -(-tk // bm) + e
    tok = jnp.broadcast_to(jnp.arange(t, dtype=I32)[:, None], (t, TOP_K))
    row_tok = jnp.zeros((n_blocks * bm,), I32).at[dest.reshape(tk)].set(tok.reshape(tk))
    block_first_row = jnp.arange(n_blocks, dtype=I32) * bm
    block_e = jnp.minimum(jnp.sum(pad_end[None, :] <= block_first_row[:, None], axis=1), e - 1).astype(I32)
    n_used = (pad_end[-1] // bm).astype(I32).reshape(1)
    return dest.astype(I32), row_tok.reshape(n_blocks, bm), block_e, n_used


def _rope_tables(n, n_ctx):
    rows = n // GRID_W
    n_freq = HEAD_DIM // 4
    inv_freq = ROPE_THETA ** (-jnp.arange(n_freq, dtype=F32) / n_freq)
    ang_r = jnp.arange(rows, dtype=F32)[:, None] * inv_freq
    ang_c = jnp.arange(GRID_W, dtype=F32)[:, None] * inv_freq

    def expand(fn):
        r = jnp.broadcast_to(fn(ang_r)[:, None, :], (rows, GRID_W, n_freq))
        c_ = jnp.broadcast_to(fn(ang_c)[None, :, :], (rows, GRID_W, n_freq))
        half = jnp.concatenate([r, c_], axis=-1).reshape(n, 2 * n_freq)
        return jnp.repeat(half, 2, axis=1)

    cos = expand(jnp.cos)
    sin = expand(jnp.sin)
    even = (jnp.arange(HEAD_DIM) % 2 == 0)[None, :]
    c = jnp.concatenate([jnp.ones((n_ctx, HEAD_DIM), F32), cos], axis=0)
    se = jnp.concatenate([jnp.zeros((n_ctx, HEAD_DIM), F32), jnp.where(even, -sin, 0.0)], axis=0)
    so = jnp.concatenate([jnp.zeros((n_ctx, HEAD_DIM), F32), jnp.where(even, 0.0, sin)], axis=0)
    return c, se, so


def kernel(x, c, ctx, c_ctx, ada_w, ada_b, norm1_g, w_in, q_norm_g, k_norm_g, na_rel_bias, out_norm_g,
           w_out, norm2_g, router_w, router_bias, exp_w_gate, exp_w_up, exp_w_down, shared_w_gate,
           shared_w_up, shared_w_down, final_g):
    b, n, d = x.shape
    l = ctx.shape[1]
    depth = ada_w.shape[0]
    assert b == 1
    xl = x[0]
    xc = ctx[0]
    mod = _adaln(jnp.stack([c_ctx, c[0]], axis=1), ada_w, ada_b)
    tabs_all = _rope_tables(n, l)
    tabs_lat = tuple(t[l:] for t in tabs_all)
    tabs_ctx = tuple(t[:l] for t in tabs_all)
    w_in_b = w_in.astype(BF16)
    w_out_b = w_out.astype(BF16)
    wgu_b = jnp.concatenate([exp_w_gate, exp_w_up], axis=-1).astype(BF16)
    wd_b = exp_w_down.astype(BF16)
    sgu_b = jnp.concatenate([shared_w_gate, shared_w_up], axis=-1).astype(BF16)
    sd_b = shared_w_down.astype(BF16)

    for layer in range(depth):
        last = layer == depth - 1
        sh1, sc1, g1, sh2, sc2, g2 = [mod[layer, 1, i * d:(i + 1) * d] for i in range(6)]
        csh1, csc1, cg1, csh2, csc2, cg2 = [mod[layer, 0, i * d:(i + 1) * d] for i in range(6)]

        h = _normmod(xl, norm1_g[layer], sc1, sh1)
        hc = _normmod(xc, norm1_g[layer], csc1, csh1)
        p = _matmul(h, w_in_b, layer, BF16, n_out=F_OFF)
        pc = _matmul(hc, w_in_b, layer, BF16, n_out=F_OFF)
        f = _matmul(h, w_in_b, layer, F32, col_block_off=F_OFF // FOURIER_WIDTH, n_out=FOURIER_WIDTH)

        qa_t = _qprep(p, tabs_lat, q_norm_g[layer])
        ka, va_t = _kvprep(pc, p, tabs_all, k_norm_g[layer])
        o_gqa = _gqa_flash(qa_t, ka, va_t)
        o_na = _na(p, pc, na_rel_bias[layer])
        o_four = _fourier(f)
        xl = _outproj(o_gqa, o_na, o_four, out_norm_g[layer], w_out_b, layer, xl, g1)

        if not last:
            fc = _matmul(hc, w_in_b, layer, F32, col_block_off=F_OFF // FOURIER_WIDTH, n_out=FOURIER_WIDTH)
            qa_ct = _qprep(pc, tabs_ctx, q_norm_g[layer])
            oc_gqa = _gqa_flash(qa_ct, ka[:l], va_t[:, :l])
            oc_na = _flash(pc, QN_OFF, pc, KN_OFF, pc, VN_OFF, NA_HEADS, 1)
            oc_four = _fourier_small(fc)
            xc = _outproj(oc_gqa, oc_na, oc_four, out_norm_g[layer], w_out_b, layer, xc, cg1)

        if last:
            h2, h2p = _normmod(xl, norm2_g[layer], sc2, sh2, packed=True)
        else:
            h2, h2p = _normmod_pair(xl, xc, norm2_g[layer], sc2, sh2, csc2, csh2)
        idx, wts, rank, counts = _router(h2, router_w[layer].astype(BF16), router_bias[layer])
        dest, row_tok, block_e, n_used = _moe_tables(idx, rank, counts, MOE_ROWS)
        ys = _experts(h2p, row_tok, block_e, n_used, wgu_b, wd_b, layer)
        ysh = _shared(h2, sgu_b, sd_b, layer)
        xl = _combine(dest[:n], wts[:n], ys, ysh, xl, g2, final_g=final_g if last else None)
        if not last:
            xc = _combine(dest[n:], wts[n:], ys, ysh, xc, cg2, ysh_row_off=n)
    return xl[None]
```
